```python
import jax, jax.numpy as jnp
from jax import lax
import numpy as np

D_MODEL = 1024
BATCH = 8
SEQ = 8192
DEPTH = 4

D_FF = 2816
NORM_EPS = 1e-5
CONV_W = 3
HEAD_SIZE = 64
N_HEADS = D_MODEL // HEAD_SIZE
D_DECAY_LORA = 64
D_AAA_LORA = 64
D_MV_LORA = 32
D_GATE_LORA = 160
GN_EPS = 64e-5
N_MIX = 6
N_CONV = (DEPTH + 1) // 2
N_RWKV = DEPTH // 2
N_VRES = max(N_RWKV - 1, 0)

kernel_name = "hybrid_shortconv_rwkv7_macaron"


def rms_norm(x, g):
    xf = x.astype(jnp.float32)
    y = xf * lax.rsqrt(jnp.mean(xf * xf, axis=-1, keepdims=True) + NORM_EPS)
    return (y * g.astype(jnp.float32)).astype(x.dtype)


def swiglu(x, w_gu, w_d):
    gate, up = jnp.split(x @ w_gu, 2, axis=-1)
    return (jax.nn.silu(gate) * up) @ w_d


def short_conv_mixer(x, w_in, conv_w, w_out):
    gb, gc, h = jnp.split(x @ w_in, 3, axis=-1)
    z = gc * h
    z = lax.conv_general_dilated(
        z, conv_w[:, None, :].astype(z.dtype), window_strides=(1,),
        padding=[(CONV_W - 1, 0)], dimension_numbers=('NWC', 'WIO', 'NWC'),
        feature_group_count=D_MODEL)
    return (gb * z) @ w_out


def wkv7_scan(r, w, k, v, a, b):
    tm = lambda t: jnp.moveaxis(t.astype(jnp.float32), 1, 0)

    def step(S, inp):
        r_t, w_t, k_t, v_t, a_t, b_t = inp
        sa = jnp.einsum('bhij,bhj->bhi', S, a_t)
        S = S * w_t[:, :, None, :] + sa[..., None] * b_t[:, :, None, :] + v_t[..., None] * k_t[:, :, None, :]
        y = jnp.einsum('bhij,bhj->bhi', S, r_t)
        return S, y

    B, _, H, N = r.shape
    S0 = jnp.zeros((B, H, N, N), jnp.float32)
    _, y = lax.scan(step, S0, (tm(r), tm(w), tm(k), tm(v), tm(a), tm(b)))
    return jnp.moveaxis(y, 0, 1)


def rwkv7_mixer(x, v_first, mu, w_rkv, w0, w1, w2, a0, a1, a2, g1, g2,
                k_k, k_a, r_k, ln_w, ln_b, w_o, v_res):
    B, S, D = x.shape
    H, N = N_HEADS, HEAD_SIZE
    xx = jnp.pad(x, ((0, 0), (1, 0), (0, 0)))[:, :-1] - x
    xs = x[None] + xx[None] * mu[:, None, None, :]
    r, k, v = jnp.einsum('pbsd,pde->pbse', xs[:3], w_rkv)
    xw, xa, xg = xs[3], xs[4], xs[5]
    w_log = -jax.nn.softplus(-(w0 + jnp.tanh(xw @ w1) @ w2).astype(jnp.float32)) - 0.5
    decay = jnp.exp(-jnp.exp(w_log))
    a = jax.nn.sigmoid(a0 + (xa @ a1) @ a2)
    g = jax.nn.sigmoid(xg @ g1) @ g2
    kk = (k * k_k).reshape(B, S, H, N).astype(jnp.float32)
    kk = kk / jnp.maximum(jnp.linalg.norm(kk, axis=-1, keepdims=True), 1e-12)
    k = k * (1 + (a - 1) * k_a)
    if v_res is None:
        v_first = v
    else:
        v0, v1, v2 = v_res
        v = v + (v_first - v) * jax.nn.sigmoid(v0 + (xs[2] @ v1) @ v2)
    rh, kh, vh = (t.reshape(B, S, H, N) for t in (r, k, v))
    ah = a.reshape(B, S, H, N).astype(jnp.float32)
    y = wkv7_scan(rh, decay.reshape(B, S, H, N), kh, vh, -kk, kk * ah)
    mean = jnp.mean(y, axis=-1, keepdims=True)
    var = jnp.mean(jnp.square(y - mean), axis=-1, keepdims=True)
    y = ((y - mean) * lax.rsqrt(var + GN_EPS)).reshape(B, S, D)
    y = y * ln_w.astype(jnp.float32) + ln_b.astype(jnp.float32)
    bonus = jnp.sum(rh.astype(jnp.float32) * kh.astype(jnp.float32) * r_k.astype(jnp.float32),
                    axis=-1, keepdims=True) * vh.astype(jnp.float32)
    y = (y + bonus.reshape(B, S, D)).astype(x.dtype)
    return (y * g) @ w_o, v_first


def setup_inputs(seed: int = 0) -> dict:
    key = jax.random.key(seed)
    ks = iter(jax.random.split(key, 40))
    nrm = lambda shape, scale: jax.random.normal(next(ks), shape, jnp.float32) * scale
    uni = lambda shape, lo, hi: jax.random.uniform(next(ks), shape, jnp.float32, lo, hi)
    D, F, H, N = D_MODEL, D_FF, N_HEADS, HEAD_SIZE
    R = N_RWKV
    return {
        "x": nrm((BATCH, SEQ, D), 1.0),
        "norm_g": 1.0 + nrm((DEPTH, 3, D), 0.02),
        "final_g": 1.0 + nrm((D,), 0.02),
        "ffn_w_gu": nrm((DEPTH, 2, D, 2 * F), D ** -0.5),
        "ffn_w_d": nrm((DEPTH, 2, F, D), F ** -0.5),
        "conv_w_in": nrm((N_CONV, D, 3 * D), D ** -0.5),
        "conv_w": nrm((N_CONV, CONV_W, D), CONV_W ** -0.5),
        "conv_w_out": nrm((N_CONV, D, D), D ** -0.5),
        "rwkv_mu": uni((R, N_MIX, D), 0.0, 1.0),
        "rwkv_w_rkv": nrm((R, 3, D, D), D ** -0.5),
        "rwkv_w0": uni((R, D), -5.0, 1.0),
        "rwkv_w1": nrm((R, D, D_DECAY_LORA), D ** -0.5),
        "rwkv_w2": nrm((R, D_DECAY_LORA, D), 0.1 * D_DECAY_LORA ** -0.5),
        "rwkv_a0": nrm((R, D), 0.1),
        "rwkv_a1": nrm((R, D, D_AAA_LORA), D ** -0.5),
        "rwkv_a2": nrm((R, D_AAA_LORA, D), 0.1 * D_AAA_LORA ** -0.5),
        "rwkv_g1": nrm((R, D, D_GATE_LORA), D ** -0.5),
        "rwkv_g2": nrm((R, D_GATE_LORA, D), D_GATE_LORA ** -0.5),
        "rwkv_k_k": 0.85 + nrm((R, D), 0.02),
        "rwkv_k_a": 1.0 + nrm((R, D), 0.02),
        "rwkv_r_k": nrm((R, H, N), 0.1),
        "rwkv_ln_w": 1.0 + nrm((R, D), 0.02),
        "rwkv_ln_b": nrm((R, D), 0.02),
        "rwkv_w_o": nrm((R, D, D), D ** -0.5),
        "rwkv_v0": 1.0 + nrm((N_VRES, D), 0.1),
        "rwkv_v1": nrm((N_VRES, D, D_MV_LORA), D ** -0.5),
        "rwkv_v2": nrm((N_VRES, D_MV_LORA, D), 0.1 * D_MV_LORA ** -0.5),
    }


def reference(x, norm_g, final_g, ffn_w_gu, ffn_w_d, conv_w_in, conv_w, conv_w_out,
              rwkv_mu, rwkv_w_rkv, rwkv_w0, rwkv_w1, rwkv_w2, rwkv_a0, rwkv_a1, rwkv_a2,
              rwkv_g1, rwkv_g2, rwkv_k_k, rwkv_k_a, rwkv_r_k, rwkv_ln_w, rwkv_ln_b, rwkv_w_o,
              rwkv_v0, rwkv_v1, rwkv_v2):
    v_first = None
    for i in range(DEPTH):
        x = x + 0.5 * swiglu(rms_norm(x, norm_g[i, 0]), ffn_w_gu[i, 0], ffn_w_d[i, 0])
        h = rms_norm(x, norm_g[i, 1])
        if i % 2 == 0:
            c = i // 2
            x = x + short_conv_mixer(h, conv_w_in[c], conv_w[c], conv_w_out[c])
        else:
            j = i // 2
            v_res = None if j == 0 else (rwkv_v0[j - 1], rwkv_v1[j - 1], rwkv_v2[j - 1])
            out, v_first = rwkv7_mixer(
                h, v_first, rwkv_mu[j], rwkv_w_rkv[j], rwkv_w0[j], rwkv_w1[j], rwkv_w2[j],
                rwkv_a0[j], rwkv_a1[j], rwkv_a2[j], rwkv_g1[j], rwkv_g2[j],
                rwkv_k_k[j], rwkv_k_a[j], rwkv_r_k[j], rwkv_ln_w[j], rwkv_ln_b[j], rwkv_w_o[j], v_res)
            x = x + out
        x = x + 0.5 * swiglu(rms_norm(x, norm_g[i, 2]), ffn_w_gu[i, 1], ffn_w_d[i, 1])
    return rms_norm(x, final_g)
```

```python
import functools
import math

import jax
import jax.numpy as jnp
from jax import lax
from jax.experimental import pallas as pl
from jax.experimental.pallas import tpu as pltpu

NORM_EPS = 1e-5
GN_EPS = 64e-5
HEAD_SIZE = 64
WKV_CHUNK = 64
WKV_GROUP = 256
HEADS_PER_GROUP = WKV_GROUP // HEAD_SIZE
WKV_BLOCK = 512
SUBLANES = 8
LANES = 128
DECAY_SCALE = math.exp(-0.5)
VMEM_LIMIT = 56 * 1024 * 1024

_MXU_DTYPE = jnp.bfloat16


def _mx(x):
    return x.astype(_MXU_DTYPE)


def _dot(a, b):
    return jnp.dot(_mx(a), _mx(b), preferred_element_type=jnp.float32)


def _dot_nt(a, b):
    return lax.dot_general(_mx(a), _mx(b), (((1,), (1,)), ((), ())),
                           preferred_element_type=jnp.float32)


def _dot_tn(a, b):
    return lax.dot_general(_mx(a), _mx(b), (((0,), (0,)), ((), ())),
                           preferred_element_type=jnp.float32)


def _rms(x, g):
    ms = jnp.mean(x * x, axis=-1, keepdims=True)
    return x * lax.rsqrt(ms + NORM_EPS) * g


def _params(*sem):
    return pltpu.CompilerParams(dimension_semantics=sem, vmem_limit_bytes=VMEM_LIMIT)


def _ffn_kernel(x_ref, g_ref, wg_ref, wu_ref, wd_ref, fg_ref, o_ref, h_ref, acc_ref,
                *, final_norm):
    j = pl.program_id(1)

    @pl.when(j == 0)
    def _():
        h_ref[...] = _mx(_rms(x_ref[...], g_ref[...]))
        acc_ref[...] = jnp.zeros_like(acc_ref)

    h = h_ref[...]
    gate = jnp.dot(h, wg_ref[...], preferred_element_type=jnp.float32)
    up = jnp.dot(h, wu_ref[...], preferred_element_type=jnp.float32)
    act = gate * jax.nn.sigmoid(gate) * up
    acc_ref[...] += _dot(act, wd_ref[...])

    @pl.when(j == pl.num_programs(1) - 1)
    def _():
        y = x_ref[...] + 0.5 * acc_ref[...]
        if final_norm:
            y = _rms(y, fg_ref[...])
        o_ref[...] = y


def _ffn(x2, g, w_gu, w_d, final_g, *, final_norm, tm, fc):
    t, d = x2.shape
    f = w_d.shape[0]
    nf = f // fc
    return pl.pallas_call(
        functools.partial(_ffn_kernel, final_norm=final_norm),
        grid=(t // tm, nf),
        in_specs=[
            pl.BlockSpec((tm, d), lambda i, j: (i, 0)),
            pl.BlockSpec((1, d), lambda i, j: (0, 0)),
            pl.BlockSpec((d, fc), lambda i, j: (0, j)),
            pl.BlockSpec((d, fc), lambda i, j: (0, j + nf)),
            pl.BlockSpec((fc, d), lambda i, j: (j, 0)),
            pl.BlockSpec((1, d), lambda i, j: (0, 0)),
        ],
        out_specs=pl.BlockSpec((tm, d), lambda i, j: (i, 0)),
        out_shape=jax.ShapeDtypeStruct((t, d), jnp.float32),
        scratch_shapes=[pltpu.VMEM((tm, d), _MXU_DTYPE), pltpu.VMEM((tm, d), jnp.float32)],
        compiler_params=_params("parallel", "arbitrary"),
        name="ffn",
    )(x2, g.reshape(1, d), w_gu, w_gu, w_d, final_g.reshape(1, d))


def _conv_kernel(x_ref, g_ref, win_ref, cw_ref, wout_ref, o_ref, zbuf_ref):
    tm, d = x_ref.shape[1], x_ref.shape[2]

    @pl.when(pl.program_id(1) == 0)
    def _():
        zbuf_ref[0:SUBLANES, :] = jnp.zeros((SUBLANES, d), jnp.float32)

    x = x_ref[0]
    h = _mx(_rms(x, g_ref[...]))
    gb = jnp.dot(h, win_ref[:, 0:d], preferred_element_type=jnp.float32)
    gc = jnp.dot(h, win_ref[:, d:2 * d], preferred_element_type=jnp.float32)
    hh = jnp.dot(h, win_ref[:, 2 * d:3 * d], preferred_element_type=jnp.float32)
    z = gc * hh
    zbuf_ref[SUBLANES:SUBLANES + tm, :] = z
    z1 = zbuf_ref[SUBLANES - 1:SUBLANES - 1 + tm, :]
    z2 = zbuf_ref[SUBLANES - 2:SUBLANES - 2 + tm, :]
    conv = cw_ref[2:3, :] * z + cw_ref[1:2, :] * z1 + cw_ref[0:1, :] * z2
    zbuf_ref[0:SUBLANES, :] = z[tm - SUBLANES:tm, :]
    o_ref[0] = x + _dot(gb * conv, wout_ref[...])


def _conv_mixer(x, g, w_in, conv_w, w_out, *, tm):
    b, s, d = x.shape
    return pl.pallas_call(
        _conv_kernel,
        grid=(b, s // tm),
        in_specs=[
            pl.BlockSpec((1, tm, d), lambda i, j: (i, j, 0)),
            pl.BlockSpec((1, d), lambda i, j: (0, 0)),
            pl.BlockSpec((d, 3 * d), lambda i, j: (0, 0)),
            pl.BlockSpec(conv_w.shape, lambda i, j: (0, 0)),
            pl.BlockSpec((d, d), lambda i, j: (0, 0)),
        ],
        out_specs=pl.BlockSpec((1, tm, d), lambda i, j: (i, j, 0)),
        out_shape=jax.ShapeDtypeStruct((b, s, d), jnp.float32),
        scratch_shapes=[pltpu.VMEM((tm + SUBLANES, d), jnp.float32)],
        compiler_params=_params("arbitrary", "arbitrary"),
        name="conv_mixer",
    )(x, g.reshape(1, d), w_in, conv_w, w_out)


def _head_sums(x, bd):
    d = x.shape[1]
    parts = [_dot(x[:, c:c + WKV_GROUP], bd) for c in range(0, d, WKV_GROUP)]
    return jnp.concatenate(parts, axis=1)


def _split3(x):
    hi = _mx(x)
    r1 = x - hi.astype(jnp.float32)
    mid = _mx(r1)
    lo = _mx(r1 - mid.astype(jnp.float32))
    return hi, mid, lo


def _proj_kernel(*refs, value_residual):
    if value_residual:
        (x_ref, g_ref, mu_ref, wr_ref, wk_ref, wv_ref, w0_ref, w1_ref, w2_ref,
         a0_ref, a1_ref, a2_ref, g1_ref, g2_ref, kk_ref, ka_ref, tri_ref, bd_ref,
         v0_ref, v1_ref, v2_ref, vf_ref,
         rt_ref, at_ref, bt_ref, kt_ref, v_ref, gate_ref, pl_ref, hbuf_ref) = refs
    else:
        (x_ref, g_ref, mu_ref, wr_ref, wk_ref, wv_ref, w0_ref, w1_ref, w2_ref,
         a0_ref, a1_ref, a2_ref, g1_ref, g2_ref, kk_ref, ka_ref, tri_ref, bd_ref,
         rt_ref, at_ref, bt_ref, kt_ref, v_ref, gate_ref, pl_ref, hbuf_ref) = refs
    tm, d = x_ref.shape[1], x_ref.shape[2]

    @pl.when(pl.program_id(1) == 0)
    def _():
        hbuf_ref[0:SUBLANES, :] = jnp.zeros((SUBLANES, d), jnp.float32)

    h = _rms(x_ref[0], g_ref[...])
    hbuf_ref[SUBLANES:SUBLANES + tm, :] = h
    xx = hbuf_ref[SUBLANES - 1:SUBLANES - 1 + tm, :] - h
    hbuf_ref[0:SUBLANES, :] = h[tm - SUBLANES:tm, :]

    def mix(p):
        return _mx(h + xx * mu_ref[p:p + 1, :])

    r = jnp.dot(mix(0), wr_ref[...], preferred_element_type=jnp.float32)
    k = jnp.dot(mix(1), wk_ref[...], preferred_element_type=jnp.float32)
    xv = mix(2)
    v = jnp.dot(xv, wv_ref[...], preferred_element_type=jnp.float32)
    zw = w0_ref[...] + _dot(jnp.tanh(jnp.dot(mix(3), w1_ref[...],
                                             preferred_element_type=jnp.float32)), w2_ref[...])
    lw = -DECAY_SCALE * jax.nn.sigmoid(zw)
    asig = jax.nn.sigmoid(a0_ref[...] + _dot(jnp.dot(mix(4), a1_ref[...],
                                                     preferred_element_type=jnp.float32),
                                             a2_ref[...]))
    gate_ref[0] = _dot(jax.nn.sigmoid(jnp.dot(mix(5), g1_ref[...],
                                              preferred_element_type=jnp.float32)), g2_ref[...])

    kk = k * kk_ref[...]
    kk = kk * lax.rsqrt(jnp.maximum(_head_sums(kk * kk, bd_ref[...]), 1e-24))
    kp = k * (1.0 + (asig - 1.0) * ka_ref[...])
    if value_residual:
        mixv = jax.nn.sigmoid(v0_ref[...] + _dot(jnp.dot(xv, v1_ref[...],
                                                         preferred_element_type=jnp.float32),
                                                 v2_ref[...]))
        v = v + (vf_ref[0] - v) * mixv
    v_ref[0] = v

    tri = tri_ref[...]
    hi, mid, lo = _split3(lw)
    c = (jnp.dot(tri, hi, preferred_element_type=jnp.float32)
         + jnp.dot(tri, mid, preferred_element_type=jnp.float32)
         + jnp.dot(tri, lo, preferred_element_type=jnp.float32))
    pinc = jnp.exp(c)
    pinv = jnp.exp(-c)
    rt_ref[0] = r * pinc
    at_ref[0] = -kk * jnp.exp(c - lw)
    bt_ref[0] = kk * asig * pinv
    kt_ref[0] = kp * pinv
    for ci in range(tm // WKV_CHUNK):
        row = ci * WKV_CHUNK + WKV_CHUNK - 1
        pl_ref[0, ci] = pinc[row:row + 1, :]


def _pad_cols(w, n):
    return jnp.pad(w, ((0, 0), (0, n - w.shape[1])))


def _pad_rows(w, n):
    return jnp.pad(w, ((0, n - w.shape[0]), (0, 0)))


def _lora_width(n):
    return -(-n // LANES) * LANES


def _rwkv_proj(x, g, mu, w_rkv, w0, w1, w2, a0, a1, a2, g1, g2, k_k, k_a, v_res, v_first, *, tm):
    b, s, d = x.shape
    row = lambda a: a.reshape(1, d)
    cst = lambda shape: pl.BlockSpec(shape, lambda i, j: (0,) * len(shape))
    tok = pl.BlockSpec((1, tm, d), lambda i, j: (i, j, 0))

    def lora(wa, wb):
        n = _lora_width(wa.shape[1])
        return _mx(_pad_cols(wa, n)), _mx(_pad_rows(wb, n))

    w1p, w2p = lora(w1, w2)
    a1p, a2p = lora(a1, a2)
    g1p, g2p = lora(g1, g2)
    idx = jnp.arange(tm)
    tri = ((idx[:, None] >= idx[None, :])
           & (idx[:, None] // WKV_CHUNK == idx[None, :] // WKV_CHUNK)).astype(_MXU_DTYPE)
    lane = jnp.arange(WKV_GROUP) // HEAD_SIZE
    bd = (lane[:, None] == lane[None, :]).astype(_MXU_DTYPE)

    args = [x, row(g), mu, _mx(w_rkv[0]), _mx(w_rkv[1]), _mx(w_rkv[2]), row(w0), w1p, w2p,
            row(a0), a1p, a2p, g1p, g2p, row(k_k), row(k_a), tri, bd]
    in_specs = [tok] + [cst(a.shape) for a in args[1:]]
    if v_res is not None:
        v0, v1, v2 = v_res
        v1p, v2p = lora(v1, v2)
        extra = [row(v0), v1p, v2p]
        args += extra + [v_first]
        in_specs += [cst(a.shape) for a in extra] + [tok]

    nchunk = tm // WKV_CHUNK
    big = jax.ShapeDtypeStruct((b, s, d), jnp.float32)
    outs = pl.pallas_call(
        functools.partial(_proj_kernel, value_residual=v_res is not None),
        grid=(b, s // tm),
        in_specs=in_specs,
        out_specs=[tok] * 6 + [pl.BlockSpec((1, nchunk, 1, d), lambda i, j: (i, j, 0, 0))],
        out_shape=[big] * 6 + [jax.ShapeDtypeStruct((b, s // WKV_CHUNK, 1, d), jnp.float32)],
        scratch_shapes=[pltpu.VMEM((tm + SUBLANES, d), jnp.float32)],
        compiler_params=_params("arbitrary", "arbitrary"),
        name="rwkv_proj",
    )(*args)
    return outs


def _stack(x, bd):
    xb = _mx(x)
    return jnp.concatenate([xb] * HEADS_PER_GROUP, axis=0) * bd


def _wkv_kernel(rt_ref, at_ref, bt_ref, kt_ref, v_ref, gate_ref, pl_ref,
                rk_ref, lnw_ref, lnb_ref, bd_ref, msk_ref, o_ref, g_ref):
    L = WKV_CHUNK

    @pl.when(pl.program_id(2) == 0)
    def _():
        g_ref[...] = jnp.zeros_like(g_ref)

    bd = bd_ref[...]
    m_strict = msk_ref[0]
    m_incl = msk_ref[1]
    eye = msk_ref[2]
    nlev = msk_ref.shape[0] - 4
    G = g_ref[...]
    for ci in range(rt_ref.shape[1] // L):
        sl = slice(ci * L, (ci + 1) * L)
        rt, at, bt, kt, v = rt_ref[0, sl, :], at_ref[0, sl, :], bt_ref[0, sl, :], kt_ref[0, sl, :], v_ref[0, sl, :]
        bs, ks, vs = _stack(bt, bd), _stack(kt, bd), _stack(v, bd)
        a_ab = _dot_nt(at, bs) * m_strict
        a_ak = _dot_nt(at, ks) * m_strict
        a_rb = _dot_nt(rt, bs) * m_incl
        a_rk = _dot_nt(rt, ks) * m_incl
        T = eye + a_ab * msk_ref[3]
        for lev in range(nlev):
            w1 = _dot(a_ab * msk_ref[4 + lev], _stack(T, bd))
            T = T + _dot(T, _stack(w1, bd))
        wm = _dot(T, _stack(at, bd))
        uv = _dot(T, _stack(_dot(a_ak, vs), bd))
        y0 = _dot(a_rk, vs)
        u = _dot_nt(wm, G) + uv
        us = _stack(u, bd)
        y = _dot_nt(rt, G) + _dot(a_rb, us) + y0
        G = (G + _dot_tn(us, bs) + _dot_tn(vs, ks)) * pl_ref[0, ci]
        mean = _dot(y, bd) * (1.0 / HEAD_SIZE)
        yc = y - mean
        var = _dot(yc * yc, bd) * (1.0 / HEAD_SIZE)
        yn = yc * lax.rsqrt(var + GN_EPS) * lnw_ref[...] + lnb_ref[...]
        bonus = _dot(rt * kt * rk_ref[...], bd) * v
        o_ref[0, sl, :] = ((yn + bonus) * gate_ref[0, sl, :]).astype(o_ref.dtype)
    g_ref[...] = G


def _wkv_masks():
    L = WKV_CHUNK
    t = jnp.arange(L)[:, None]
    s = (jnp.arange(WKV_GROUP) % L)[None, :]
    masks = [s < t, s <= t, s == t]
    n = 1
    while n < L:
        masks.append((s < t) & (t // (2 * n) == s // (2 * n)) & (t // n != s // n))
        n *= 2
    return jnp.stack(masks).astype(jnp.float32)


def _wkv(rt, at, bt, kt, v, gate, pl_arr, r_k, ln_w, ln_b):
    b, s, d = rt.shape
    blk = min(WKV_BLOCK, s)
    nchunk = blk // WKV_CHUNK
    lane = jnp.arange(WKV_GROUP) // HEAD_SIZE
    rowh = jnp.arange(HEADS_PER_GROUP * WKV_CHUNK) // WKV_CHUNK
    bd = (rowh[:, None] == lane[None, :]).astype(_MXU_DTYPE)
    masks = _wkv_masks()
    tok = pl.BlockSpec((1, blk, WKV_GROUP), lambda i, j, k: (i, k, j))
    par = pl.BlockSpec((1, WKV_GROUP), lambda i, j, k: (0, j))
    return pl.pallas_call(
        _wkv_kernel,
        grid=(b, d // WKV_GROUP, s // blk),
        in_specs=[tok] * 6 + [
            pl.BlockSpec((1, nchunk, 1, WKV_GROUP), lambda i, j, k: (i, k, 0, j)),
            par, par, par,
            pl.BlockSpec(bd.shape, lambda i, j, k: (0, 0)),
            pl.BlockSpec(masks.shape, lambda i, j, k: (0, 0, 0)),
        ],
        out_specs=tok,
        out_shape=jax.ShapeDtypeStruct((b, s, d), _MXU_DTYPE),
        scratch_shapes=[pltpu.VMEM((WKV_GROUP, WKV_GROUP), jnp.float32)],
        compiler_params=_params("arbitrary", "arbitrary", "arbitrary"),
        name="wkv7",
    )(rt, at, bt, kt, v, gate, pl_arr, r_k.reshape(1, d), ln_w.reshape(1, d), ln_b.reshape(1, d),
      bd, masks)


def _out_kernel(x_ref, y_ref, w_ref, o_ref):
    o_ref[...] = x_ref[...] + jnp.dot(y_ref[...], w_ref[...], preferred_element_type=jnp.float32)


def _out_proj(x2, y2, w, *, tm):
    t, d = x2.shape
    return pl.pallas_call(
        _out_kernel,
        grid=(t // tm,),
        in_specs=[pl.BlockSpec((tm, d), lambda i: (i, 0)),
                  pl.BlockSpec((tm, d), lambda i: (i, 0)),
                  pl.BlockSpec((d, d), lambda i: (0, 0))],
        out_specs=pl.BlockSpec((tm, d), lambda i: (i, 0)),
        out_shape=jax.ShapeDtypeStruct((t, d), jnp.float32),
        compiler_params=_params("parallel"),
        name="out_proj",
    )(x2, y2, w)


def _tile(n, want):
    return min(n, want)


def kernel(x, norm_g, final_g, ffn_w_gu, ffn_w_d, conv_w_in, conv_w, conv_w_out, rwkv_mu, rwkv_w_rkv, rwkv_w0, rwkv_w1, rwkv_w2, rwkv_a0, rwkv_a1, rwkv_a2, rwkv_g1, rwkv_g2, rwkv_k_k, rwkv_k_a, rwkv_r_k, rwkv_ln_w, rwkv_ln_b, rwkv_w_o, rwkv_v0, rwkv_v1, rwkv_v2):
    b, s, d = x.shape
    depth = norm_g.shape[0]
    f = ffn_w_d.shape[2]
    assert d % WKV_GROUP == 0 and s % WKV_CHUNK == 0
    t = b * s
    ffn_tm = _tile(t, 1024)
    ffn_fc = 256 if f % 256 == 0 else f
    w_gu = _mx(ffn_w_gu)
    w_d = _mx(ffn_w_d)

    def ffn(x3, i, half, final_norm=False):
        y = _ffn(x3.reshape(t, d), norm_g[i, 2 * half], w_gu[i, half], w_d[i, half], final_g,
                 final_norm=final_norm, tm=ffn_tm, fc=ffn_fc)
        return y.reshape(b, s, d)

    v_first = None
    for i in range(depth):
        x = ffn(x, i, 0)
        if i % 2 == 0:
            c = i // 2
            x = _conv_mixer(x, norm_g[i, 1], _mx(conv_w_in[c]), conv_w[c], _mx(conv_w_out[c]),
                            tm=_tile(s, 512))
        else:
            j = i // 2
            v_res = None if j == 0 else (rwkv_v0[j - 1], rwkv_v1[j - 1], rwkv_v2[j - 1])
            rt, at, bt, kt, v, gate, pl_arr = _rwkv_proj(
                x, norm_g[i, 1], rwkv_mu[j], rwkv_w_rkv[j], rwkv_w0[j], rwkv_w1[j], rwkv_w2[j],
                rwkv_a0[j], rwkv_a1[j], rwkv_a2[j], rwkv_g1[j], rwkv_g2[j],
                rwkv_k_k[j], rwkv_k_a[j], v_res, v_first, tm=_tile(s, 256))
            if j == 0:
                v_first = v
            yg = _wkv(rt, at, bt, kt, v, gate, pl_arr, rwkv_r_k[j].reshape(-1),
                      rwkv_ln_w[j], rwkv_ln_b[j])
            x = _out_proj(x.reshape(t, d), yg.reshape(t, d), _mx(rwkv_w_o[j]),
                          tm=_tile(t, 1024)).reshape(b, s, d)
        x = ffn(x, i, 1, final_norm=(i == depth - 1))
    return x
```

```python
import functools
import math

import jax
import jax.numpy as jnp
from jax import lax
from jax.experimental import pallas as pl
from jax.experimental.pallas import tpu as pltpu

NORM_EPS = 1e-5
GN_EPS = 64e-5
HEAD_SIZE = 64
WKV_CHUNK = 64
WKV_GROUP = 256
HEADS_PER_GROUP = WKV_GROUP // HEAD_SIZE
WKV_BLOCK = 512
SUBLANES = 8
LANES = 128
DECAY_SCALE = math.exp(-0.5)
VMEM_LIMIT = 56 * 1024 * 1024

_MXU_DTYPE = jnp.bfloat16


def _mx(x):
    return x.astype(_MXU_DTYPE)


def _dot(a, b):
    return jnp.dot(_mx(a), _mx(b), preferred_element_type=jnp.float32)


def _dot_nt(a, b):
    return lax.dot_general(_mx(a), _mx(b), (((1,), (1,)), ((), ())),
                           preferred_element_type=jnp.float32)


def _dot_tn(a, b):
    return lax.dot_general(_mx(a), _mx(b), (((0,), (0,)), ((), ())),
                           preferred_element_type=jnp.float32)


def _rms(x, g):
    ms = jnp.mean(x * x, axis=-1, keepdims=True)
    return x * lax.rsqrt(ms + NORM_EPS) * g


def _params(*sem):
    return pltpu.CompilerParams(dimension_semantics=sem, vmem_limit_bytes=VMEM_LIMIT)


def _ffn_kernel(x_ref, g_ref, wg_ref, wu_ref, wd_ref, fg_ref, o_ref, h_ref, acc_ref,
                *, final_norm):
    j = pl.program_id(1)

    @pl.when(j == 0)
    def _():
        h_ref[...] = _mx(_rms(x_ref[...], g_ref[...]))
        acc_ref[...] = jnp.zeros_like(acc_ref)

    h = h_ref[...]
    gate = jnp.dot(h, wg_ref[...], preferred_element_type=jnp.float32)
    up = jnp.dot(h, wu_ref[...], preferred_element_type=jnp.float32)
    act = gate * jax.nn.sigmoid(gate) * up
    acc_ref[...] += _dot(act, wd_ref[...])

    @pl.when(j == pl.num_programs(1) - 1)
    def _():
        y = x_ref[...] + 0.5 * acc_ref[...]
        if final_norm:
            y = _rms(y, fg_ref[...])
        o_ref[...] = y


def _ffn(x2, g, w_gu, w_d, final_g, *, final_norm, tm, fc):
    t, d = x2.shape
    f = w_d.shape[0]
    nf = f // fc
    return pl.pallas_call(
        functools.partial(_ffn_kernel, final_norm=final_norm),
        grid=(t // tm, nf),
        in_specs=[
            pl.BlockSpec((tm, d), lambda i, j: (i, 0)),
            pl.BlockSpec((1, d), lambda i, j: (0, 0)),
            pl.BlockSpec((d, fc), lambda i, j: (0, j)),
            pl.BlockSpec((d, fc), lambda i, j: (0, j + nf)),
            pl.BlockSpec((fc, d), lambda i, j: (j, 0)),
            pl.BlockSpec((1, d), lambda i, j: (0, 0)),
        ],
        out_specs=pl.BlockSpec((tm, d), lambda i, j: (i, 0)),
        out_shape=jax.ShapeDtypeStruct((t, d), jnp.float32),
        scratch_shapes=[pltpu.VMEM((tm, d), _MXU_DTYPE), pltpu.VMEM((tm, d), jnp.float32)],
        compiler_params=_params("parallel", "arbitrary"),
        name="ffn",
    )(x2, g.reshape(1, d), w_gu, w_gu, w_d, final_g.reshape(1, d))


def _conv_kernel(x_ref, g_ref, win_ref, cw_ref, wout_ref, o_ref, zbuf_ref):
    tm, d = x_ref.shape[1], x_ref.shape[2]

    @pl.when(pl.program_id(1) == 0)
    def _():
        zbuf_ref[0:SUBLANES, :] = jnp.zeros((SUBLANES, d), jnp.float32)

    x = x_ref[0]
    h = _mx(_rms(x, g_ref[...]))
    gb = jnp.dot(h, win_ref[:, 0:d], preferred_element_type=jnp.float32)
    gc = jnp.dot(h, win_ref[:, d:2 * d], preferred_element_type=jnp.float32)
    hh = jnp.dot(h, win_ref[:, 2 * d:3 * d], preferred_element_type=jnp.float32)
    z = gc * hh
    zbuf_ref[SUBLANES:SUBLANES + tm, :] = z
    z1 = zbuf_ref[SUBLANES - 1:SUBLANES - 1 + tm, :]
    z2 = zbuf_ref[SUBLANES - 2:SUBLANES - 2 + tm, :]
    conv = cw_ref[2:3, :] * z + cw_ref[1:2, :] * z1 + cw_ref[0:1, :] * z2
    zbuf_ref[0:SUBLANES, :] = z[tm - SUBLANES:tm, :]
    o_ref[0] = x + _dot(gb * conv, wout_ref[...])


def _conv_mixer(x, g, w_in, conv_w, w_out, *, tm):
    b, s, d = x.shape
    return pl.pallas_call(
        _conv_kernel,
        grid=(b, s // tm),
        in_specs=[
            pl.BlockSpec((1, tm, d), lambda i, j: (i, j, 0)),
            pl.BlockSpec((1, d), lambda i, j: (0, 0)),
            pl.BlockSpec((d, 3 * d), lambda i, j: (0, 0)),
            pl.BlockSpec(conv_w.shape, lambda i, j: (0, 0)),
            pl.BlockSpec((d, d), lambda i, j: (0, 0)),
        ],
        out_specs=pl.BlockSpec((1, tm, d), lambda i, j: (i, j, 0)),
        out_shape=jax.ShapeDtypeStruct((b, s, d), jnp.float32),
        scratch_shapes=[pltpu.VMEM((tm + SUBLANES, d), jnp.float32)],
        compiler_params=_params("arbitrary", "arbitrary"),
        name="conv_mixer",
    )(x, g.reshape(1, d), w_in, conv_w, w_out)


def _head_sums(x, bd):
    d = x.shape[1]
    parts = [_dot(x[:, c:c + WKV_GROUP], bd) for c in range(0, d, WKV_GROUP)]
    return jnp.concatenate(parts, axis=1)


def _split3(x):
    hi = _mx(x)
    r1 = x - hi.astype(jnp.float32)
    mid = _mx(r1)
    lo = _mx(r1 - mid.astype(jnp.float32))
    return hi, mid, lo


def _proj_kernel(*refs, value_residual):
    if value_residual:
        (x_ref, g_ref, mu_ref, wr_ref, wk_ref, wv_ref, w0_ref, w1_ref, w2_ref,
         a0_ref, a1_ref, a2_ref, g1_ref, g2_ref, kk_ref, ka_ref, tri_ref, bd_ref,
         v0_ref, v1_ref, v2_ref, vf_ref,
         rt_ref, at_ref, bt_ref, kt_ref, v_ref, gate_ref, pl_ref, hbuf_ref) = refs
    else:
        (x_ref, g_ref, mu_ref, wr_ref, wk_ref, wv_ref, w0_ref, w1_ref, w2_ref,
         a0_ref, a1_ref, a2_ref, g1_ref, g2_ref, kk_ref, ka_ref, tri_ref, bd_ref,
         rt_ref, at_ref, bt_ref, kt_ref, v_ref, gate_ref, pl_ref, hbuf_ref) = refs
    tm, d = x_ref.shape[1], x_ref.shape[2]

    @pl.when(pl.program_id(1) == 0)
    def _():
        hbuf_ref[0:SUBLANES, :] = jnp.zeros((SUBLANES, d), jnp.float32)

    h = _rms(x_ref[0], g_ref[...])
    hbuf_ref[SUBLANES:SUBLANES + tm, :] = h
    xx = hbuf_ref[SUBLANES - 1:SUBLANES - 1 + tm, :] - h
    hbuf_ref[0:SUBLANES, :] = h[tm - SUBLANES:tm, :]

    def mix(p):
        return _mx(h + xx * mu_ref[p:p + 1, :])

    r = jnp.dot(mix(0), wr_ref[...], preferred_element_type=jnp.float32)
    k = jnp.dot(mix(1), wk_ref[...], preferred_element_type=jnp.float32)
    xv = mix(2)
    v = jnp.dot(xv, wv_ref[...], preferred_element_type=jnp.float32)
    zw = w0_ref[...] + _dot(jnp.tanh(jnp.dot(mix(3), w1_ref[...],
                                             preferred_element_type=jnp.float32)), w2_ref[...])
    lw = -DECAY_SCALE * jax.nn.sigmoid(zw)
    asig = jax.nn.sigmoid(a0_ref[...] + _dot(jnp.dot(mix(4), a1_ref[...],
                                                     preferred_element_type=jnp.float32),
                                             a2_ref[...]))
    gate_ref[0] = _dot(jax.nn.sigmoid(jnp.dot(mix(5), g1_ref[...],
                                              preferred_element_type=jnp.float32)), g2_ref[...])

    kk = k * kk_ref[...]
    kk = kk * lax.rsqrt(jnp.maximum(_head_sums(kk * kk, bd_ref[...]), 1e-24))
    kp = k * (1.0 + (asig - 1.0) * ka_ref[...])
    if value_residual:
        mixv = jax.nn.sigmoid(v0_ref[...] + _dot(jnp.dot(xv, v1_ref[...],
                                                         preferred_element_type=jnp.float32),
                                                 v2_ref[...]))
        v = v + (vf_ref[0] - v) * mixv
    v_ref[0] = v

    tri = tri_ref[...]
    hi, mid, lo = _split3(lw)
    c = (jnp.dot(tri, hi, preferred_element_type=jnp.float32)
         + jnp.dot(tri, mid, preferred_element_type=jnp.float32)
         + jnp.dot(tri, lo, preferred_element_type=jnp.float32))
    pinc = jnp.exp(c)
    pinv = jnp.exp(-c)
    rt_ref[0] = r * pinc
    at_ref[0] = -kk * jnp.exp(c - lw)
    bt_ref[0] = kk * asig * pinv
    kt_ref[0] = kp * pinv
    for ci in range(tm // WKV_CHUNK):
        row = ci * WKV_CHUNK + WKV_CHUNK - 1
        pl_ref[0, ci] = pinc[row:row + 1, :]


def _pad_cols(w, n):
    return jnp.pad(w, ((0, 0), (0, n - w.shape[1])))


def _pad_rows(w, n):
    return jnp.pad(w, ((0, n - w.shape[0]), (0, 0)))


def _lora_width(n):
    return -(-n // LANES) * LANES


def _rwkv_proj(x, g, mu, w_rkv, w0, w1, w2, a0, a1, a2, g1, g2, k_k, k_a, v_res, v_first, *, tm):
    b, s, d = x.shape
    row = lambda a: a.reshape(1, d)
    cst = lambda shape: pl.BlockSpec(shape, lambda i, j: (0,) * len(shape))
    tok = pl.BlockSpec((1, tm, d), lambda i, j: (i, j, 0))

    def lora(wa, wb):
        n = _lora_width(wa.shape[1])
        return _mx(_pad_cols(wa, n)), _mx(_pad_rows(wb, n))

    w1p, w2p = lora(w1, w2)
    a1p, a2p = lora(a1, a2)
    g1p, g2p = lora(g1, g2)
    idx = jnp.arange(tm)
    tri = ((idx[:, None] >= idx[None, :])
           & (idx[:, None] // WKV_CHUNK == idx[None, :] // WKV_CHUNK)).astype(_MXU_DTYPE)
    lane = jnp.arange(WKV_GROUP) // HEAD_SIZE
    bd = (lane[:, None] == lane[None, :]).astype(_MXU_DTYPE)

    args = [x, row(g), mu, _mx(w_rkv[0]), _mx(w_rkv[1]), _mx(w_rkv[2]), row(w0), w1p, w2p,
            row(a0), a1p, a2p, g1p, g2p, row(k_k), row(k_a), tri, bd]
    in_specs = [tok] + [cst(a.shape) for a in args[1:]]
    if v_res is not None:
        v0, v1, v2 = v_res
        v1p, v2p = lora(v1, v2)
        extra = [row(v0), v1p, v2p]
        args += extra + [v_first]
        in_specs += [cst(a.shape) for a in extra] + [tok]

    nchunk = tm // WKV_CHUNK
    big = jax.ShapeDtypeStruct((b, s, d), jnp.float32)
    outs = pl.pallas_call(
        functools.partial(_proj_kernel, value_residual=v_res is not None),
        grid=(b, s // tm),
        in_specs=in_specs,
        out_specs=[tok] * 6 + [pl.BlockSpec((1, nchunk, 1, d), lambda i, j: (i, j, 0, 0))],
        out_shape=[big] * 6 + [jax.ShapeDtypeStruct((b, s // WKV_CHUNK, 1, d), jnp.float32)],
        scratch_shapes=[pltpu.VMEM((tm + SUBLANES, d), jnp.float32)],
        compiler_params=_params("arbitrary", "arbitrary"),
        name="rwkv_proj",
    )(*args)
    return outs


def _stack(x, bd):
    xb = _mx(x)
    return jnp.concatenate([xb] * HEADS_PER_GROUP, axis=0) * bd


def _wkv_kernel(rt_ref, at_ref, bt_ref, kt_ref, v_ref, gate_ref, pl_ref,
                rk_ref, lnw_ref, lnb_ref, bd_ref, msk_ref, o_ref, g_ref):
    L = WKV_CHUNK

    @pl.when(pl.program_id(2) == 0)
    def _():
        g_ref[...] = jnp.zeros_like(g_ref)

    bd = bd_ref[...]
    m_strict = msk_ref[0]
    m_incl = msk_ref[1]
    eye = msk_ref[2]
    nlev = msk_ref.shape[0] - 4
    W = WKV_GROUP
    chunks = range(rt_ref.shape[1] // L)
    st = lambda x: _stack(x, bd)
    rows = lambda ref, c: ref[0, c * L:(c + 1) * L, :]

    rt = [rows(rt_ref, c) for c in chunks]
    at = [rows(at_ref, c) for c in chunks]
    bs = [st(rows(bt_ref, c)) for c in chunks]
    ks = [st(rows(kt_ref, c)) for c in chunks]
    vs = [st(rows(v_ref, c)) for c in chunks]
    bk = [jnp.concatenate([bs[c], ks[c]], axis=0) for c in chunks]
    aa = [_dot_nt(jnp.concatenate([at[c], rt[c]], axis=0), bk[c]) for c in chunks]
    a_ab = [aa[c][:L, :W] for c in chunks]
    a_ak = [aa[c][:L, W:] * m_strict for c in chunks]
    a_rb = [aa[c][L:, :W] * m_incl for c in chunks]
    a_rk = [aa[c][L:, W:] * m_incl for c in chunks]
    T = [eye + a_ab[c] * msk_ref[3] for c in chunks]
    for lev in range(nlev):
        w1 = [_dot(a_ab[c] * msk_ref[4 + lev], st(T[c])) for c in chunks]
        T = [T[c] + _dot(T[c], st(w1[c])) for c in chunks]
    avy = [_dot(jnp.concatenate([a_ak[c], a_rk[c]], axis=0), vs[c]) for c in chunks]
    swm = [st(_dot(T[c], st(at[c]))) for c in chunks]
    suv = [st(_dot(T[c], st(avy[c][:L]))) for c in chunks]
    delta = [_dot_tn(swm[c], bs[c]) for c in chunks]
    nn = [_dot_tn(jnp.concatenate([suv[c], vs[c]], axis=0), bk[c]) for c in chunks]
    qy = [_dot(a_rb[c], jnp.concatenate([swm[c], suv[c]], axis=1)) for c in chunks]
    q = [rt[c] + qy[c][:, :W] for c in chunks]
    y1 = [qy[c][:, W:] + avy[c][L:] for c in chunks]

    G = g_ref[...]
    ys = []
    for c in chunks:
        ys.append(_dot_nt(q[c], G) + y1[c])
        G = (G + _dot(G, delta[c]) + nn[c]) * pl_ref[0, c]
    g_ref[...] = G

    y = jnp.concatenate(ys, axis=0)
    mean = _dot(y, bd) * (1.0 / HEAD_SIZE)
    yc = y - mean
    var = _dot(yc * yc, bd) * (1.0 / HEAD_SIZE)
    yn = yc * lax.rsqrt(var + GN_EPS) * lnw_ref[...] + lnb_ref[...]
    bonus = _dot(rt_ref[0] * kt_ref[0] * rk_ref[...], bd) * v_ref[0]
    o_ref[0] = ((yn + bonus) * gate_ref[0]).astype(o_ref.dtype)


def _wkv_masks():
    L = WKV_CHUNK
    t = jnp.arange(L)[:, None]
    s = (jnp.arange(WKV_GROUP) % L)[None, :]
    masks = [s < t, s <= t, s == t]
    n = 1
    while n < L:
        masks.append((s < t) & (t // (2 * n) == s // (2 * n)) & (t // n != s // n))
        n *= 2
    return jnp.stack(masks).astype(jnp.float32)


def _wkv(rt, at, bt, kt, v, gate, pl_arr, r_k, ln_w, ln_b):
    b, s, d = rt.shape
    blk = min(WKV_BLOCK, s)
    nchunk = blk // WKV_CHUNK
    lane = jnp.arange(WKV_GROUP) // HEAD_SIZE
    rowh = jnp.arange(HEADS_PER_GROUP * WKV_CHUNK) // WKV_CHUNK
    bd = (rowh[:, None] == lane[None, :]).astype(_MXU_DTYPE)
    masks = _wkv_masks()
    tok = pl.BlockSpec((1, blk, WKV_GROUP), lambda i, j, k: (i, k, j))
    par = pl.BlockSpec((1, WKV_GROUP), lambda i, j, k: (0, j))
    return pl.pallas_call(
        _wkv_kernel,
        grid=(b, d // WKV_GROUP, s // blk),
        in_specs=[tok] * 6 + [
            pl.BlockSpec((1, nchunk, 1, WKV_GROUP), lambda i, j, k: (i, k, 0, j)),
            par, par, par,
            pl.BlockSpec(bd.shape, lambda i, j, k: (0, 0)),
            pl.BlockSpec(masks.shape, lambda i, j, k: (0, 0, 0)),
        ],
        out_specs=tok,
        out_shape=jax.ShapeDtypeStruct((b, s, d), _MXU_DTYPE),
        scratch_shapes=[pltpu.VMEM((WKV_GROUP, WKV_GROUP), jnp.float32)],
        compiler_params=_params("arbitrary", "arbitrary", "arbitrary"),
        name="wkv7",
    )(rt, at, bt, kt, v, gate, pl_arr, r_k.reshape(1, d), ln_w.reshape(1, d), ln_b.reshape(1, d),
      bd, masks)


def _out_kernel(x_ref, y_ref, w_ref, o_ref):
    o_ref[...] = x_ref[...] + jnp.dot(y_ref[...], w_ref[...], preferred_element_type=jnp.float32)


def _out_proj(x2, y2, w, *, tm):
    t, d = x2.shape
    return pl.pallas_call(
        _out_kernel,
        grid=(t // tm,),
        in_specs=[pl.BlockSpec((tm, d), lambda i: (i, 0)),
                  pl.BlockSpec((tm, d), lambda i: (i, 0)),
                  pl.BlockSpec((d, d), lambda i: (0, 0))],
        out_specs=pl.BlockSpec((tm, d), lambda i: (i, 0)),
        out_shape=jax.ShapeDtypeStruct((t, d), jnp.float32),
        compiler_params=_params("parallel"),
        name="out_proj",
    )(x2, y2, w)


def _tile(n, want):
    return min(n, want)


def kernel(x, norm_g, final_g, ffn_w_gu, ffn_w_d, conv_w_in, conv_w, conv_w_out, rwkv_mu, rwkv_w_rkv, rwkv_w0, rwkv_w1, rwkv_w2, rwkv_a0, rwkv_a1, rwkv_a2, rwkv_g1, rwkv_g2, rwkv_k_k, rwkv_k_a, rwkv_r_k, rwkv_ln_w, rwkv_ln_b, rwkv_w_o, rwkv_v0, rwkv_v1, rwkv_v2):
    b, s, d = x.shape
    depth = norm_g.shape[0]
    f = ffn_w_d.shape[2]
    assert d % WKV_GROUP == 0 and s % WKV_CHUNK == 0
    t = b * s
    ffn_tm = _tile(t, 1024)
    ffn_fc = 256 if f % 256 == 0 else f
    w_gu = _mx(ffn_w_gu)
    w_d = _mx(ffn_w_d)

    def ffn(x3, i, half, final_norm=False):
        y = _ffn(x3.reshape(t, d), norm_g[i, 2 * half], w_gu[i, half], w_d[i, half], final_g,
                 final_norm=final_norm, tm=ffn_tm, fc=ffn_fc)
        return y.reshape(b, s, d)

    v_first = None
    for i in range(depth):
        x = ffn(x, i, 0)
        if i % 2 == 0:
            c = i // 2
            x = _conv_mixer(x, norm_g[i, 1], _mx(conv_w_in[c]), conv_w[c], _mx(conv_w_out[c]),
                            tm=_tile(s, 512))
        else:
            j = i // 2
            v_res = None if j == 0 else (rwkv_v0[j - 1], rwkv_v1[j - 1], rwkv_v2[j - 1])
            rt, at, bt, kt, v, gate, pl_arr = _rwkv_proj(
                x, norm_g[i, 1], rwkv_mu[j], rwkv_w_rkv[j], rwkv_w0[j], rwkv_w1[j], rwkv_w2[j],
                rwkv_a0[j], rwkv_a1[j], rwkv_a2[j], rwkv_g1[j], rwkv_g2[j],
                rwkv_k_k[j], rwkv_k_a[j], v_res, v_first, tm=_tile(s, 256))
            if j == 0:
                v_first = v
            yg = _wkv(rt, at, bt, kt, v, gate, pl_arr, rwkv_r_k[j].reshape(-1),
                      rwkv_ln_w[j], rwkv_ln_b[j])
            x = _out_proj(x.reshape(t, d), yg.reshape(t, d), _mx(rwkv_w_o[j]),
                          tm=_tile(t, 1024)).reshape(b, s, d)
        x = ffn(x, i, 1, final_norm=(i == depth - 1))
    return x
```

```python
import functools
import math

import jax
import jax.numpy as jnp
from jax import lax
from jax.experimental import pallas as pl
from jax.experimental.pallas import tpu as pltpu

NORM_EPS = 1e-5
GN_EPS = 64e-5
HEAD_SIZE = 64
WKV_CHUNK = 64
WKV_GROUP = 256
HEADS_PER_GROUP = WKV_GROUP // HEAD_SIZE
WKV_GROUPS_PER_STEP = 2
WKV_BLOCK = 512
FFN_CHUNK = 512
SUBLANES = 8
LANES = 128
DECAY_SCALE = math.exp(-0.5)
VMEM_LIMIT = 56 * 1024 * 1024

_MXU_DTYPE = jnp.bfloat16


def _mx(x):
    return x.astype(_MXU_DTYPE)


def _dot(a, b):
    return jnp.dot(_mx(a), _mx(b), preferred_element_type=jnp.float32)


def _dot_nt(a, b):
    return lax.dot_general(_mx(a), _mx(b), (((1,), (1,)), ((), ())),
                           preferred_element_type=jnp.float32)


def _dot_tn(a, b):
    return lax.dot_general(_mx(a), _mx(b), (((0,), (0,)), ((), ())),
                           preferred_element_type=jnp.float32)


def _rms(x, g):
    ms = jnp.mean(x * x, axis=-1, keepdims=True)
    return x * lax.rsqrt(ms + NORM_EPS) * g


def _params(*sem):
    return pltpu.CompilerParams(dimension_semantics=sem, vmem_limit_bytes=VMEM_LIMIT)


def _ffn_kernel(*refs, mixer_out, final_norm, chunks):
    if mixer_out:
        x_ref, y_ref, wo_ref, g_ref, wgu_ref, wd_ref, fg_ref, o_ref, act_ref = refs
        x = x_ref[...] + jnp.dot(y_ref[...], wo_ref[...], preferred_element_type=jnp.float32)
    else:
        x_ref, g_ref, wgu_ref, wd_ref, fg_ref, o_ref, act_ref = refs
        x = x_ref[...]
    f = wd_ref.shape[0]
    h = _mx(_rms(x, g_ref[...]))
    for lo, hi in chunks:
        gate = jnp.dot(h, wgu_ref[:, lo:hi], preferred_element_type=jnp.float32)
        up = jnp.dot(h, wgu_ref[:, f + lo:f + hi], preferred_element_type=jnp.float32)
        act_ref[:, lo:hi] = _mx(gate * jax.nn.sigmoid(gate) * up)
    y = x + 0.5 * jnp.dot(act_ref[...], wd_ref[...], preferred_element_type=jnp.float32)
    if final_norm:
        y = _rms(y, fg_ref[...])
    o_ref[...] = y


def _resident(shape):
    return pl.BlockSpec(shape, lambda i: (0,) * len(shape), pipeline_mode=pl.Buffered(1))


def _ffn(x2, g, w_gu, w_d, final_g, y2=None, w_o=None, *, final_norm, tm):
    t, d = x2.shape
    f = w_d.shape[0]
    chunks = tuple((lo, min(lo + FFN_CHUNK, f)) for lo in range(0, f, FFN_CHUNK))
    tok = pl.BlockSpec((tm, d), lambda i: (i, 0))
    mixer_out = y2 is not None
    args, in_specs = [x2], [tok]
    if mixer_out:
        args += [y2, w_o]
        in_specs += [tok, _resident((d, d))]
    args += [g.reshape(1, d), w_gu, w_d, final_g.reshape(1, d)]
    in_specs += [_resident((1, d)), _resident((d, 2 * f)), _resident((f, d)), _resident((1, d))]
    return pl.pallas_call(
        functools.partial(_ffn_kernel, mixer_out=mixer_out, final_norm=final_norm, chunks=chunks),
        grid=(t // tm,),
        in_specs=in_specs,
        out_specs=tok,
        out_shape=jax.ShapeDtypeStruct((t, d), jnp.float32),
        scratch_shapes=[pltpu.VMEM((tm, f), _MXU_DTYPE)],
        compiler_params=_params("parallel"),
        name="ffn",
    )(*args)


def _conv_kernel(x_ref, g_ref, win_ref, cw_ref, o_ref, zbuf_ref):
    tm, d = x_ref.shape[1], x_ref.shape[2]

    @pl.when(pl.program_id(1) == 0)
    def _():
        zbuf_ref[0:SUBLANES, :] = jnp.zeros((SUBLANES, d), jnp.float32)

    h = _mx(_rms(x_ref[0], g_ref[...]))
    gb = jnp.dot(h, win_ref[:, 0:d], preferred_element_type=jnp.float32)
    gc = jnp.dot(h, win_ref[:, d:2 * d], preferred_element_type=jnp.float32)
    hh = jnp.dot(h, win_ref[:, 2 * d:3 * d], preferred_element_type=jnp.float32)
    z = gc * hh
    zbuf_ref[SUBLANES:SUBLANES + tm, :] = z
    z1 = zbuf_ref[SUBLANES - 1:SUBLANES - 1 + tm, :]
    z2 = zbuf_ref[SUBLANES - 2:SUBLANES - 2 + tm, :]
    conv = cw_ref[2:3, :] * z + cw_ref[1:2, :] * z1 + cw_ref[0:1, :] * z2
    zbuf_ref[0:SUBLANES, :] = z[tm - SUBLANES:tm, :]
    o_ref[0] = _mx(gb * conv)


def _conv_mixer(x, g, w_in, conv_w, *, tm):
    b, s, d = x.shape
    return pl.pallas_call(
        _conv_kernel,
        grid=(b, s // tm),
        in_specs=[
            pl.BlockSpec((1, tm, d), lambda i, j: (i, j, 0)),
            pl.BlockSpec((1, d), lambda i, j: (0, 0)),
            pl.BlockSpec((d, 3 * d), lambda i, j: (0, 0)),
            pl.BlockSpec(conv_w.shape, lambda i, j: (0, 0)),
        ],
        out_specs=pl.BlockSpec((1, tm, d), lambda i, j: (i, j, 0)),
        out_shape=jax.ShapeDtypeStruct((b, s, d), _MXU_DTYPE),
        scratch_shapes=[pltpu.VMEM((tm + SUBLANES, d), jnp.float32)],
        compiler_params=_params("arbitrary", "arbitrary"),
        name="conv_mixer",
    )(x, g.reshape(1, d), w_in, conv_w)


def _head_sums(x, bd):
    d = x.shape[1]
    parts = [_dot(x[:, c:c + WKV_GROUP], bd) for c in range(0, d, WKV_GROUP)]
    return jnp.concatenate(parts, axis=1)


def _split3(x):
    hi = _mx(x)
    r1 = x - hi.astype(jnp.float32)
    mid = _mx(r1)
    lo = _mx(r1 - mid.astype(jnp.float32))
    return hi, mid, lo


def _proj_kernel(*refs, value_residual):
    if value_residual:
        (x_ref, g_ref, mu_ref, wr_ref, wk_ref, wv_ref, w0_ref, w1_ref, w2_ref,
         a0_ref, a1_ref, a2_ref, g1_ref, g2_ref, kk_ref, ka_ref, tri_ref, bd_ref,
         v0_ref, v1_ref, v2_ref, vf_ref,
         rt_ref, at_ref, bt_ref, kt_ref, v_ref, gate_ref, pl_ref, hbuf_ref) = refs
    else:
        (x_ref, g_ref, mu_ref, wr_ref, wk_ref, wv_ref, w0_ref, w1_ref, w2_ref,
         a0_ref, a1_ref, a2_ref, g1_ref, g2_ref, kk_ref, ka_ref, tri_ref, bd_ref,
         rt_ref, at_ref, bt_ref, kt_ref, v_ref, gate_ref, pl_ref, vf_ref, hbuf_ref) = refs
    tm, d = x_ref.shape[1], x_ref.shape[2]

    @pl.when(pl.program_id(1) == 0)
    def _():
        hbuf_ref[0:SUBLANES, :] = jnp.zeros((SUBLANES, d), jnp.float32)

    h = _rms(x_ref[0], g_ref[...])
    hbuf_ref[SUBLANES:SUBLANES + tm, :] = h
    xx = hbuf_ref[SUBLANES - 1:SUBLANES - 1 + tm, :] - h
    hbuf_ref[0:SUBLANES, :] = h[tm - SUBLANES:tm, :]

    def mix(p):
        return _mx(h + xx * mu_ref[p:p + 1, :])

    r = jnp.dot(mix(0), wr_ref[...], preferred_element_type=jnp.float32)
    k = jnp.dot(mix(1), wk_ref[...], preferred_element_type=jnp.float32)
    xv = mix(2)
    v = jnp.dot(xv, wv_ref[...], preferred_element_type=jnp.float32)
    zw = w0_ref[...] + _dot(jnp.tanh(jnp.dot(mix(3), w1_ref[...],
                                             preferred_element_type=jnp.float32)), w2_ref[...])
    lw = -DECAY_SCALE * jax.nn.sigmoid(zw)
    asig = jax.nn.sigmoid(a0_ref[...] + _dot(jnp.dot(mix(4), a1_ref[...],
                                                     preferred_element_type=jnp.float32),
                                             a2_ref[...]))
    gate_ref[0] = _mx(_dot(jax.nn.sigmoid(jnp.dot(mix(5), g1_ref[...],
                                                  preferred_element_type=jnp.float32)),
                           g2_ref[...]))

    kk = k * kk_ref[...]
    kk = kk * lax.rsqrt(jnp.maximum(_head_sums(kk * kk, bd_ref[...]), 1e-24))
    kp = k * (1.0 + (asig - 1.0) * ka_ref[...])
    if value_residual:
        mixv = jax.nn.sigmoid(v0_ref[...] + _dot(jnp.dot(xv, v1_ref[...],
                                                         preferred_element_type=jnp.float32),
                                                 v2_ref[...]))
        v = v + (vf_ref[0] - v) * mixv
    else:
        vf_ref[0] = v
    v_ref[0] = _mx(v)

    tri = tri_ref[...]
    hi, mid, lo = _split3(lw)
    c = (jnp.dot(tri, hi, preferred_element_type=jnp.float32)
         + jnp.dot(tri, mid, preferred_element_type=jnp.float32)
         + jnp.dot(tri, lo, preferred_element_type=jnp.float32))
    pinc = jnp.exp(c)
    pinv = jnp.exp(-c)
    rt_ref[0] = _mx(r * pinc)
    at_ref[0] = _mx(-kk * jnp.exp(c - lw))
    bt_ref[0] = _mx(kk * asig * pinv)
    kt_ref[0] = _mx(kp * pinv)
    for ci in range(tm // WKV_CHUNK):
        row = ci * WKV_CHUNK + WKV_CHUNK - 1
        pl_ref[0, ci] = pinc[row:row + 1, :]


def _pad_cols(w, n):
    return jnp.pad(w, ((0, 0), (0, n - w.shape[1])))


def _pad_rows(w, n):
    return jnp.pad(w, ((0, n - w.shape[0]), (0, 0)))


def _lora_width(n):
    return -(-n // LANES) * LANES


def _rwkv_proj(x, g, mu, w_rkv, w0, w1, w2, a0, a1, a2, g1, g2, k_k, k_a, v_res, v_first, *, tm):
    b, s, d = x.shape
    row = lambda a: a.reshape(1, d)
    cst = lambda shape: pl.BlockSpec(shape, lambda i, j: (0,) * len(shape))
    tok = pl.BlockSpec((1, tm, d), lambda i, j: (i, j, 0))

    def lora(wa, wb):
        n = _lora_width(wa.shape[1])
        return _mx(_pad_cols(wa, n)), _mx(_pad_rows(wb, n))

    w1p, w2p = lora(w1, w2)
    a1p, a2p = lora(a1, a2)
    g1p, g2p = lora(g1, g2)
    idx = jnp.arange(tm)
    tri = ((idx[:, None] >= idx[None, :])
           & (idx[:, None] // WKV_CHUNK == idx[None, :] // WKV_CHUNK)).astype(_MXU_DTYPE)
    lane = jnp.arange(WKV_GROUP) // HEAD_SIZE
    bd = (lane[:, None] == lane[None, :]).astype(_MXU_DTYPE)

    args = [x, row(g), mu, _mx(w_rkv[0]), _mx(w_rkv[1]), _mx(w_rkv[2]), row(w0), w1p, w2p,
            row(a0), a1p, a2p, g1p, g2p, row(k_k), row(k_a), tri, bd]
    in_specs = [tok] + [cst(a.shape) for a in args[1:]]
    if v_res is not None:
        v0, v1, v2 = v_res
        v1p, v2p = lora(v1, v2)
        extra = [row(v0), v1p, v2p]
        args += extra + [v_first]
        in_specs += [cst(a.shape) for a in extra] + [tok]

    nchunk = tm // WKV_CHUNK
    out_specs = [tok] * 6 + [pl.BlockSpec((1, nchunk, 1, d), lambda i, j: (i, j, 0, 0))]
    out_shape = ([jax.ShapeDtypeStruct((b, s, d), _MXU_DTYPE)] * 6
                 + [jax.ShapeDtypeStruct((b, s // WKV_CHUNK, 1, d), jnp.float32)])
    if v_res is None:
        out_specs.append(tok)
        out_shape.append(jax.ShapeDtypeStruct((b, s, d), jnp.float32))
    outs = pl.pallas_call(
        functools.partial(_proj_kernel, value_residual=v_res is not None),
        grid=(b, s // tm),
        in_specs=in_specs,
        out_specs=out_specs,
        out_shape=out_shape,
        scratch_shapes=[pltpu.VMEM((tm + SUBLANES, d), jnp.float32)],
        compiler_params=_params("arbitrary", "arbitrary"),
        name="rwkv_proj",
    )(*args)
    return outs


def _stack(x, bd):
    xb = _mx(x)
    return jnp.concatenate([xb] * HEADS_PER_GROUP, axis=0) * bd


def _wkv_kernel(rt_ref, at_ref, bt_ref, kt_ref, v_ref, gate_ref, pl_ref,
                rk_ref, lnw_ref, lnb_ref, bd_ref, msk_ref, o_ref, g_ref):
    L = WKV_CHUNK

    @pl.when(pl.program_id(2) == 0)
    def _():
        g_ref[...] = jnp.zeros_like(g_ref)

    bd = bd_ref[...]
    m_strict = msk_ref[0]
    m_incl = msk_ref[1]
    eye = msk_ref[2]
    nlev = msk_ref.shape[0] - 4
    W = WKV_GROUP
    nchunk = rt_ref.shape[1] // L
    ngroup = rt_ref.shape[2] // W
    items = [(c, g) for c in range(nchunk) for g in range(ngroup)]
    n = range(len(items))
    st = lambda x: _stack(x, bd)
    tile = lambda ref, i: ref[0, items[i][0] * L:(items[i][0] + 1) * L,
                              items[i][1] * W:(items[i][1] + 1) * W]

    rt = [tile(rt_ref, i) for i in n]
    at = [tile(at_ref, i) for i in n]
    bs = [st(tile(bt_ref, i)) for i in n]
    ks = [st(tile(kt_ref, i)) for i in n]
    vs = [st(tile(v_ref, i)) for i in n]
    bk = [jnp.concatenate([bs[i], ks[i]], axis=0) for i in n]
    aa = [_dot_nt(jnp.concatenate([at[i], rt[i]], axis=0), bk[i]) for i in n]
    a_ab = [aa[i][:L, :W] for i in n]
    a_ak = [aa[i][:L, W:] * m_strict for i in n]
    a_rb = [aa[i][L:, :W] * m_incl for i in n]
    a_rk = [aa[i][L:, W:] * m_incl for i in n]
    T = [eye + a_ab[i] * msk_ref[3] for i in n]
    for lev in range(nlev):
        w1 = [_dot(a_ab[i] * msk_ref[4 + lev], st(T[i])) for i in n]
        T = [T[i] + _dot(T[i], st(w1[i])) for i in n]
    avy = [_dot(jnp.concatenate([a_ak[i], a_rk[i]], axis=0), vs[i]) for i in n]
    swm = [st(_dot(T[i], st(at[i]))) for i in n]
    suv = [st(_dot(T[i], st(avy[i][:L]))) for i in n]
    delta = [_dot_tn(swm[i], bs[i]) for i in n]
    nn = [_dot_tn(jnp.concatenate([suv[i], vs[i]], axis=0), bk[i]) for i in n]
    qy = [_dot(a_rb[i], jnp.concatenate([swm[i], suv[i]], axis=1)) for i in n]
    q = [rt[i] + qy[i][:, :W] for i in n]
    y1 = [qy[i][:, W:] + avy[i][L:] for i in n]

    G = [g_ref[g] for g in range(ngroup)]
    ys = []
    for i in n:
        c, g = items[i]
        ys.append(_dot_nt(q[i], G[g]) + y1[i])
        G[g] = (G[g] + _dot(G[g], delta[i]) + nn[i]) * pl_ref[0, c][:, g * W:(g + 1) * W]
    for g in range(ngroup):
        g_ref[g] = G[g]

    for g in range(ngroup):
        lanes = slice(g * W, (g + 1) * W)
        y = jnp.concatenate([ys[i] for i in n if items[i][1] == g], axis=0)
        mean = _dot(y, bd) * (1.0 / HEAD_SIZE)
        yc = y - mean
        var = _dot(yc * yc, bd) * (1.0 / HEAD_SIZE)
        yn = yc * lax.rsqrt(var + GN_EPS) * lnw_ref[:, lanes] + lnb_ref[:, lanes]
        rkk = rt_ref[0, :, lanes].astype(jnp.float32) * kt_ref[0, :, lanes] * rk_ref[:, lanes]
        bonus = _dot(rkk, bd) * v_ref[0, :, lanes]
        o_ref[0, :, lanes] = ((yn + bonus) * gate_ref[0, :, lanes]).astype(o_ref.dtype)


def _wkv_masks():
    L = WKV_CHUNK
    t = jnp.arange(L)[:, None]
    s = (jnp.arange(WKV_GROUP) % L)[None, :]
    masks = [s < t, s <= t, s == t]
    n = 1
    while n < L:
        masks.append((s < t) & (t // (2 * n) == s // (2 * n)) & (t // n != s // n))
        n *= 2
    return jnp.stack(masks).astype(jnp.float32)


def _wkv(rt, at, bt, kt, v, gate, pl_arr, r_k, ln_w, ln_b):
    b, s, d = rt.shape
    blk = min(WKV_BLOCK, s)
    nchunk = blk // WKV_CHUNK
    lane = jnp.arange(WKV_GROUP) // HEAD_SIZE
    rowh = jnp.arange(HEADS_PER_GROUP * WKV_CHUNK) // WKV_CHUNK
    bd = (rowh[:, None] == lane[None, :]).astype(_MXU_DTYPE)
    masks = _wkv_masks()
    width = WKV_GROUPS_PER_STEP * WKV_GROUP
    tok = pl.BlockSpec((1, blk, width), lambda i, j, k: (i, k, j))
    par = pl.BlockSpec((1, width), lambda i, j, k: (0, j))
    return pl.pallas_call(
        _wkv_kernel,
        grid=(b, d // width, s // blk),
        in_specs=[tok] * 6 + [
            pl.BlockSpec((1, nchunk, 1, width), lambda i, j, k: (i, k, 0, j)),
            par, par, par,
            pl.BlockSpec(bd.shape, lambda i, j, k: (0, 0)),
            pl.BlockSpec(masks.shape, lambda i, j, k: (0, 0, 0)),
        ],
        out_specs=tok,
        out_shape=jax.ShapeDtypeStruct((b, s, d), _MXU_DTYPE),
        scratch_shapes=[pltpu.VMEM((WKV_GROUPS_PER_STEP, WKV_GROUP, WKV_GROUP), jnp.float32)],
        compiler_params=_params("arbitrary", "arbitrary", "arbitrary"),
        name="wkv7",
    )(rt, at, bt, kt, v, gate, pl_arr, r_k.reshape(1, d), ln_w.reshape(1, d), ln_b.reshape(1, d),
      bd, masks)


def _tile(n, want):
    return min(n, want)


def kernel(x, norm_g, final_g, ffn_w_gu, ffn_w_d, conv_w_in, conv_w, conv_w_out, rwkv_mu, rwkv_w_rkv, rwkv_w0, rwkv_w1, rwkv_w2, rwkv_a0, rwkv_a1, rwkv_a2, rwkv_g1, rwkv_g2, rwkv_k_k, rwkv_k_a, rwkv_r_k, rwkv_ln_w, rwkv_ln_b, rwkv_w_o, rwkv_v0, rwkv_v1, rwkv_v2):
    b, s, d = x.shape
    depth = norm_g.shape[0]
    f = ffn_w_d.shape[2]
    assert d % (WKV_GROUPS_PER_STEP * WKV_GROUP) == 0 and s % WKV_CHUNK == 0
    t = b * s
    ffn_tm = _tile(t, 1024)
    w_gu = _mx(ffn_w_gu)
    w_d = _mx(ffn_w_d)

    def ffn(x3, i, half, mix=None, final_norm=False):
        y2, w_o = (None, None) if mix is None else (mix[0].reshape(t, d), mix[1])
        y = _ffn(x3.reshape(t, d), norm_g[i, 2 * half], w_gu[i, half], w_d[i, half], final_g,
                 y2, w_o, final_norm=final_norm, tm=ffn_tm)
        return y.reshape(b, s, d)

    v_first = None
    for i in range(depth):
        x = ffn(x, i, 0)
        if i % 2 == 0:
            c = i // 2
            y = _conv_mixer(x, norm_g[i, 1], _mx(conv_w_in[c]), conv_w[c], tm=_tile(s, 512))
            w_o = _mx(conv_w_out[c])
        else:
            j = i // 2
            v_res = None if j == 0 else (rwkv_v0[j - 1], rwkv_v1[j - 1], rwkv_v2[j - 1])
            outs = _rwkv_proj(
                x, norm_g[i, 1], rwkv_mu[j], rwkv_w_rkv[j], rwkv_w0[j], rwkv_w1[j], rwkv_w2[j],
                rwkv_a0[j], rwkv_a1[j], rwkv_a2[j], rwkv_g1[j], rwkv_g2[j],
                rwkv_k_k[j], rwkv_k_a[j], v_res, v_first, tm=_tile(s, 256))
            rt, at, bt, kt, v, gate, pl_arr = outs[:7]
            if j == 0:
                v_first = outs[7]
            y = _wkv(rt, at, bt, kt, v, gate, pl_arr, rwkv_r_k[j].reshape(-1),
                     rwkv_ln_w[j], rwkv_ln_b[j])
            w_o = _mx(rwkv_w_o[j])
        x = ffn(x, i, 1, mix=(y, w_o), final_norm=(i == depth - 1))
    return x
```

```python
import functools
import math

import jax
import jax.numpy as jnp
from jax import lax
from jax.experimental import pallas as pl
from jax.experimental.pallas import tpu as pltpu

NORM_EPS = 1e-5
GN_EPS = 64e-5
HEAD_SIZE = 64
WKV_CHUNK = 64
WKV_GROUP = 256
HEADS_PER_GROUP = WKV_GROUP // HEAD_SIZE
WKV_PREP_BLOCK = 512
WKV_PREP_GROUPS = 2
WKV_SCAN_BLOCK = 256
WKV_SCAN_GROUPS = 4
FFN_CHUNK = 512
SUBLANES = 8
LANES = 128
DECAY_SCALE = math.exp(-0.5)
VMEM_LIMIT = 56 * 1024 * 1024

_MXU_DTYPE = jnp.bfloat16


def _mx(x):
    return x.astype(_MXU_DTYPE)


def _dot(a, b):
    return jnp.dot(_mx(a), _mx(b), preferred_element_type=jnp.float32)


def _dot_nt(a, b):
    return lax.dot_general(_mx(a), _mx(b), (((1,), (1,)), ((), ())),
                           preferred_element_type=jnp.float32)


def _dot_tn(a, b):
    return lax.dot_general(_mx(a), _mx(b), (((0,), (0,)), ((), ())),
                           preferred_element_type=jnp.float32)


def _rms(x, g):
    ms = jnp.mean(x * x, axis=-1, keepdims=True)
    return x * lax.rsqrt(ms + NORM_EPS) * g


def _params(*sem):
    return pltpu.CompilerParams(dimension_semantics=sem, vmem_limit_bytes=VMEM_LIMIT)


def _ffn_kernel(*refs, mixer_out, final_norm, chunks):
    if mixer_out:
        x_ref, y_ref, wo_ref, g_ref, wgu_ref, wd_ref, fg_ref, o_ref, act_ref = refs
        x = x_ref[...] + jnp.dot(y_ref[...], wo_ref[...], preferred_element_type=jnp.float32)
    else:
        x_ref, g_ref, wgu_ref, wd_ref, fg_ref, o_ref, act_ref = refs
        x = x_ref[...]
    f = wd_ref.shape[0]
    h = _mx(_rms(x, g_ref[...]))
    for lo, hi in chunks:
        gate = jnp.dot(h, wgu_ref[:, lo:hi], preferred_element_type=jnp.float32)
        up = jnp.dot(h, wgu_ref[:, f + lo:f + hi], preferred_element_type=jnp.float32)
        act_ref[:, lo:hi] = _mx(gate * jax.nn.sigmoid(gate) * up)
    y = x + 0.5 * jnp.dot(act_ref[...], wd_ref[...], preferred_element_type=jnp.float32)
    if final_norm:
        y = _rms(y, fg_ref[...])
    o_ref[...] = y


def _resident(shape):
    return pl.BlockSpec(shape, lambda i: (0,) * len(shape), pipeline_mode=pl.Buffered(1))


def _ffn(x2, g, w_gu, w_d, final_g, y2=None, w_o=None, *, final_norm, tm):
    t, d = x2.shape
    f = w_d.shape[0]
    chunks = tuple((lo, min(lo + FFN_CHUNK, f)) for lo in range(0, f, FFN_CHUNK))
    tok = pl.BlockSpec((tm, d), lambda i: (i, 0))
    mixer_out = y2 is not None
    args, in_specs = [x2], [tok]
    if mixer_out:
        args += [y2, w_o]
        in_specs += [tok, _resident((d, d))]
    args += [g.reshape(1, d), w_gu, w_d, final_g.reshape(1, d)]
    in_specs += [_resident((1, d)), _resident((d, 2 * f)), _resident((f, d)), _resident((1, d))]
    return pl.pallas_call(
        functools.partial(_ffn_kernel, mixer_out=mixer_out, final_norm=final_norm, chunks=chunks),
        grid=(t // tm,),
        in_specs=in_specs,
        out_specs=tok,
        out_shape=jax.ShapeDtypeStruct((t, d), jnp.float32),
        scratch_shapes=[pltpu.VMEM((tm, f), _MXU_DTYPE)],
        compiler_params=_params("parallel"),
        name="ffn",
    )(*args)


def _conv_kernel(x_ref, g_ref, win_ref, cw_ref, o_ref, zbuf_ref):
    tm, d = x_ref.shape[1], x_ref.shape[2]

    @pl.when(pl.program_id(1) == 0)
    def _():
        zbuf_ref[0:SUBLANES, :] = jnp.zeros((SUBLANES, d), jnp.float32)

    h = _mx(_rms(x_ref[0], g_ref[...]))
    gb = jnp.dot(h, win_ref[:, 0:d], preferred_element_type=jnp.float32)
    gc = jnp.dot(h, win_ref[:, d:2 * d], preferred_element_type=jnp.float32)
    hh = jnp.dot(h, win_ref[:, 2 * d:3 * d], preferred_element_type=jnp.float32)
    z = gc * hh
    zbuf_ref[SUBLANES:SUBLANES + tm, :] = z
    z1 = zbuf_ref[SUBLANES - 1:SUBLANES - 1 + tm, :]
    z2 = zbuf_ref[SUBLANES - 2:SUBLANES - 2 + tm, :]
    conv = cw_ref[2:3, :] * z + cw_ref[1:2, :] * z1 + cw_ref[0:1, :] * z2
    zbuf_ref[0:SUBLANES, :] = z[tm - SUBLANES:tm, :]
    o_ref[0] = _mx(gb * conv)


def _conv_mixer(x, g, w_in, conv_w, *, tm):
    b, s, d = x.shape
    return pl.pallas_call(
        _conv_kernel,
        grid=(b, s // tm),
        in_specs=[
            pl.BlockSpec((1, tm, d), lambda i, j: (i, j, 0)),
            pl.BlockSpec((1, d), lambda i, j: (0, 0)),
            pl.BlockSpec((d, 3 * d), lambda i, j: (0, 0)),
            pl.BlockSpec(conv_w.shape, lambda i, j: (0, 0)),
        ],
        out_specs=pl.BlockSpec((1, tm, d), lambda i, j: (i, j, 0)),
        out_shape=jax.ShapeDtypeStruct((b, s, d), _MXU_DTYPE),
        scratch_shapes=[pltpu.VMEM((tm + SUBLANES, d), jnp.float32)],
        compiler_params=_params("arbitrary", "arbitrary"),
        name="conv_mixer",
    )(x, g.reshape(1, d), w_in, conv_w)


def _head_sums(x, bd):
    d = x.shape[1]
    parts = [_dot(x[:, c:c + WKV_GROUP], bd) for c in range(0, d, WKV_GROUP)]
    return jnp.concatenate(parts, axis=1)


def _split3(x):
    hi = _mx(x)
    r1 = x - hi.astype(jnp.float32)
    mid = _mx(r1)
    lo = _mx(r1 - mid.astype(jnp.float32))
    return hi, mid, lo


def _proj_kernel(*refs, value_residual):
    if value_residual:
        (x_ref, g_ref, mu_ref, wr_ref, wk_ref, wv_ref, w0_ref, w1_ref, w2_ref,
         a0_ref, a1_ref, a2_ref, g1_ref, g2_ref, kk_ref, ka_ref, tri_ref, bd_ref,
         v0_ref, v1_ref, v2_ref, vf_ref,
         rt_ref, at_ref, bt_ref, kt_ref, v_ref, gate_ref, pl_ref, hbuf_ref) = refs
    else:
        (x_ref, g_ref, mu_ref, wr_ref, wk_ref, wv_ref, w0_ref, w1_ref, w2_ref,
         a0_ref, a1_ref, a2_ref, g1_ref, g2_ref, kk_ref, ka_ref, tri_ref, bd_ref,
         rt_ref, at_ref, bt_ref, kt_ref, v_ref, gate_ref, pl_ref, vf_ref, hbuf_ref) = refs
    tm, d = x_ref.shape[1], x_ref.shape[2]

    @pl.when(pl.program_id(1) == 0)
    def _():
        hbuf_ref[0:SUBLANES, :] = jnp.zeros((SUBLANES, d), jnp.float32)

    h = _rms(x_ref[0], g_ref[...])
    hbuf_ref[SUBLANES:SUBLANES + tm, :] = h
    xx = hbuf_ref[SUBLANES - 1:SUBLANES - 1 + tm, :] - h
    hbuf_ref[0:SUBLANES, :] = h[tm - SUBLANES:tm, :]

    def mix(p):
        return _mx(h + xx * mu_ref[p:p + 1, :])

    r = jnp.dot(mix(0), wr_ref[...], preferred_element_type=jnp.float32)
    k = jnp.dot(mix(1), wk_ref[...], preferred_element_type=jnp.float32)
    xv = mix(2)
    v = jnp.dot(xv, wv_ref[...], preferred_element_type=jnp.float32)
    zw = w0_ref[...] + _dot(jnp.tanh(jnp.dot(mix(3), w1_ref[...],
                                             preferred_element_type=jnp.float32)), w2_ref[...])
    lw = -DECAY_SCALE * jax.nn.sigmoid(zw)
    asig = jax.nn.sigmoid(a0_ref[...] + _dot(jnp.dot(mix(4), a1_ref[...],
                                                     preferred_element_type=jnp.float32),
                                             a2_ref[...]))
    gate_ref[0] = _mx(_dot(jax.nn.sigmoid(jnp.dot(mix(5), g1_ref[...],
                                                  preferred_element_type=jnp.float32)),
                           g2_ref[...]))

    kk = k * kk_ref[...]
    kk = kk * lax.rsqrt(jnp.maximum(_head_sums(kk * kk, bd_ref[...]), 1e-24))
    kp = k * (1.0 + (asig - 1.0) * ka_ref[...])
    if value_residual:
        mixv = jax.nn.sigmoid(v0_ref[...] + _dot(jnp.dot(xv, v1_ref[...],
                                                         preferred_element_type=jnp.float32),
                                                 v2_ref[...]))
        v = v + (vf_ref[0] - v) * mixv
    else:
        vf_ref[0] = v
    v_ref[0] = _mx(v)

    tri = tri_ref[...]
    hi, mid, lo = _split3(lw)
    c = (jnp.dot(tri, hi, preferred_element_type=jnp.float32)
         + jnp.dot(tri, mid, preferred_element_type=jnp.float32)
         + jnp.dot(tri, lo, preferred_element_type=jnp.float32))
    pinc = jnp.exp(c)
    pinv = jnp.exp(-c)
    rt_ref[0] = _mx(r * pinc)
    at_ref[0] = _mx(-kk * jnp.exp(c - lw))
    bt_ref[0] = _mx(kk * asig * pinv)
    kt_ref[0] = _mx(kp * pinv)
    for ci in range(tm // WKV_CHUNK):
        row = ci * WKV_CHUNK + WKV_CHUNK - 1
        pl_ref[0, ci] = pinc[row:row + 1, :]


def _pad_cols(w, n):
    return jnp.pad(w, ((0, 0), (0, n - w.shape[1])))


def _pad_rows(w, n):
    return jnp.pad(w, ((0, n - w.shape[0]), (0, 0)))


def _lora_width(n):
    return -(-n // LANES) * LANES


def _rwkv_proj(x, g, mu, w_rkv, w0, w1, w2, a0, a1, a2, g1, g2, k_k, k_a, v_res, v_first, *, tm):
    b, s, d = x.shape
    row = lambda a: a.reshape(1, d)
    cst = lambda shape: pl.BlockSpec(shape, lambda i, j: (0,) * len(shape))
    tok = pl.BlockSpec((1, tm, d), lambda i, j: (i, j, 0))

    def lora(wa, wb):
        n = _lora_width(wa.shape[1])
        return _mx(_pad_cols(wa, n)), _mx(_pad_rows(wb, n))

    w1p, w2p = lora(w1, w2)
    a1p, a2p = lora(a1, a2)
    g1p, g2p = lora(g1, g2)
    idx = jnp.arange(tm)
    tri = ((idx[:, None] >= idx[None, :])
           & (idx[:, None] // WKV_CHUNK == idx[None, :] // WKV_CHUNK)).astype(_MXU_DTYPE)
    lane = jnp.arange(WKV_GROUP) // HEAD_SIZE
    bd = (lane[:, None] == lane[None, :]).astype(_MXU_DTYPE)

    args = [x, row(g), mu, _mx(w_rkv[0]), _mx(w_rkv[1]), _mx(w_rkv[2]), row(w0), w1p, w2p,
            row(a0), a1p, a2p, g1p, g2p, row(k_k), row(k_a), tri, bd]
    in_specs = [tok] + [cst(a.shape) for a in args[1:]]
    if v_res is not None:
        v0, v1, v2 = v_res
        v1p, v2p = lora(v1, v2)
        extra = [row(v0), v1p, v2p]
        args += extra + [v_first]
        in_specs += [cst(a.shape) for a in extra] + [tok]

    nchunk = tm // WKV_CHUNK
    out_specs = [tok] * 6 + [pl.BlockSpec((1, nchunk, 1, d), lambda i, j: (i, j, 0, 0))]
    out_shape = ([jax.ShapeDtypeStruct((b, s, d), _MXU_DTYPE)] * 6
                 + [jax.ShapeDtypeStruct((b, s // WKV_CHUNK, 1, d), jnp.float32)])
    if v_res is None:
        out_specs.append(tok)
        out_shape.append(jax.ShapeDtypeStruct((b, s, d), jnp.float32))
    outs = pl.pallas_call(
        functools.partial(_proj_kernel, value_residual=v_res is not None),
        grid=(b, s // tm),
        in_specs=in_specs,
        out_specs=out_specs,
        out_shape=out_shape,
        scratch_shapes=[pltpu.VMEM((tm + SUBLANES, d), jnp.float32)],
        compiler_params=_params("arbitrary", "arbitrary"),
        name="rwkv_proj",
    )(*args)
    return outs


def _stack(x, bd):
    xb = _mx(x)
    return jnp.concatenate([xb] * HEADS_PER_GROUP, axis=0) * bd


def _wkv_prep_kernel(rt_ref, at_ref, bt_ref, kt_ref, v_ref, bd_ref, msk_ref,
                     wm_ref, uv_ref, y0_ref, arb_ref):
    L, W = WKV_CHUNK, WKV_GROUP
    bd = bd_ref[...]
    m_strict = msk_ref[0]
    m_incl = msk_ref[1]
    eye = msk_ref[2]
    nlev = msk_ref.shape[0] - 4
    items = [(c, g) for c in range(rt_ref.shape[1] // L) for g in range(rt_ref.shape[2] // W)]
    n = range(len(items))
    st = lambda x: _stack(x, bd)
    where = lambda i: (0, slice(items[i][0] * L, (items[i][0] + 1) * L),
                       slice(items[i][1] * W, (items[i][1] + 1) * W))

    rt = [rt_ref[where(i)] for i in n]
    at = [at_ref[where(i)] for i in n]
    vs = [st(v_ref[where(i)]) for i in n]
    bk = [jnp.concatenate([st(bt_ref[where(i)]), st(kt_ref[where(i)])], axis=0) for i in n]
    aa = [_dot_nt(jnp.concatenate([at[i], rt[i]], axis=0), bk[i]) for i in n]
    a_ab = [aa[i][:L, :W] for i in n]
    a_ak = [aa[i][:L, W:] * m_strict for i in n]
    a_rk = [aa[i][L:, W:] * m_incl for i in n]
    for i in n:
        arb_ref[where(i)] = _mx(aa[i][L:, :W] * m_incl)
    T = [eye + a_ab[i] * msk_ref[3] for i in n]
    for lev in range(nlev):
        w1 = [_dot(a_ab[i] * msk_ref[4 + lev], st(T[i])) for i in n]
        T = [T[i] + _dot(T[i], st(w1[i])) for i in n]
    avy = [_dot(jnp.concatenate([a_ak[i], a_rk[i]], axis=0), vs[i]) for i in n]
    for i in n:
        wm_ref[where(i)] = _mx(_dot(T[i], st(at[i])))
        uv_ref[where(i)] = _dot(T[i], st(avy[i][:L]))
        y0_ref[where(i)] = avy[i][L:]


def _wkv_scan_kernel(wm_ref, uv_ref, y0_ref, arb_ref, rt_ref, bt_ref, kt_ref, v_ref, gate_ref,
                     pl_ref, rk_ref, lnw_ref, lnb_ref, bd_ref, o_ref, g_ref):
    L, W = WKV_CHUNK, WKV_GROUP
    nchunk = rt_ref.shape[1] // L
    ngroup = rt_ref.shape[2] // W

    @pl.when(pl.program_id(2) == 0)
    def _():
        g_ref[...] = jnp.zeros_like(g_ref)

    bd = bd_ref[...]
    st = lambda x: _stack(x, bd)
    groups = range(ngroup)
    G = [g_ref[g] for g in groups]
    ys = [[] for _ in groups]
    for c in range(nchunk):
        for g in groups:
            at_ = (0, slice(c * L, (c + 1) * L), slice(g * W, (g + 1) * W))
            bk = jnp.concatenate([st(bt_ref[at_]), st(kt_ref[at_])], axis=0)
            ur = _dot_nt(jnp.concatenate([wm_ref[at_], rt_ref[at_]], axis=0), G[g])
            us = st(ur[:L] + uv_ref[at_])
            ys[g].append(ur[L:] + _dot(arb_ref[at_], us) + y0_ref[at_])
            upd = _dot_tn(jnp.concatenate([us, st(v_ref[at_])], axis=0), bk)
            G[g] = (G[g] + upd) * pl_ref[0, c][:, g * W:(g + 1) * W]
    for g in groups:
        g_ref[g] = G[g]

    for g in groups:
        lanes = slice(g * W, (g + 1) * W)
        y = jnp.concatenate(ys[g], axis=0)
        mean = _dot(y, bd) * (1.0 / HEAD_SIZE)
        yc = y - mean
        var = _dot(yc * yc, bd) * (1.0 / HEAD_SIZE)
        yn = yc * lax.rsqrt(var + GN_EPS) * lnw_ref[:, lanes] + lnb_ref[:, lanes]
        rkk = rt_ref[0, :, lanes].astype(jnp.float32) * kt_ref[0, :, lanes] * rk_ref[:, lanes]
        bonus = _dot(rkk, bd) * v_ref[0, :, lanes]
        o_ref[0, :, lanes] = ((yn + bonus) * gate_ref[0, :, lanes]).astype(o_ref.dtype)


def _wkv_masks():
    L = WKV_CHUNK
    t = jnp.arange(L)[:, None]
    s = (jnp.arange(WKV_GROUP) % L)[None, :]
    masks = [s < t, s <= t, s == t]
    n = 1
    while n < L:
        masks.append((s < t) & (t // (2 * n) == s // (2 * n)) & (t // n != s // n))
        n *= 2
    return jnp.stack(masks).astype(jnp.float32)


def _wkv(rt, at, bt, kt, v, gate, pl_arr, r_k, ln_w, ln_b):
    b, s, d = rt.shape
    lane = jnp.arange(WKV_GROUP) // HEAD_SIZE
    rowh = jnp.arange(HEADS_PER_GROUP * WKV_CHUNK) // WKV_CHUNK
    bd = (rowh[:, None] == lane[None, :]).astype(_MXU_DTYPE)
    masks = _wkv_masks()

    blk = min(WKV_PREP_BLOCK, s)
    width = WKV_PREP_GROUPS * WKV_GROUP
    tok = pl.BlockSpec((1, blk, width), lambda i, j, k: (i, k, j))
    shape = lambda dt: jax.ShapeDtypeStruct((b, s, d), dt)
    wm, uv, y0, arb = pl.pallas_call(
        _wkv_prep_kernel,
        grid=(b, d // width, s // blk),
        in_specs=[tok] * 5 + [pl.BlockSpec(bd.shape, lambda i, j, k: (0, 0)),
                              pl.BlockSpec(masks.shape, lambda i, j, k: (0, 0, 0))],
        out_specs=[tok] * 4,
        out_shape=[shape(_MXU_DTYPE), shape(jnp.float32), shape(jnp.float32), shape(_MXU_DTYPE)],
        compiler_params=_params("parallel", "parallel", "parallel"),
        name="wkv7_prep",
    )(rt, at, bt, kt, v, bd, masks)

    blk = min(WKV_SCAN_BLOCK, s)
    width = WKV_SCAN_GROUPS * WKV_GROUP
    tok = pl.BlockSpec((1, blk, width), lambda i, j, k: (i, k, j))
    par = pl.BlockSpec((1, width), lambda i, j, k: (0, j))
    return pl.pallas_call(
        _wkv_scan_kernel,
        grid=(b, d // width, s // blk),
        in_specs=[tok] * 9 + [
            pl.BlockSpec((1, blk // WKV_CHUNK, 1, width), lambda i, j, k: (i, k, 0, j)),
            par, par, par,
            pl.BlockSpec(bd.shape, lambda i, j, k: (0, 0)),
        ],
        out_specs=tok,
        out_shape=jax.ShapeDtypeStruct((b, s, d), _MXU_DTYPE),
        scratch_shapes=[pltpu.VMEM((WKV_SCAN_GROUPS, WKV_GROUP, WKV_GROUP), jnp.float32)],
        compiler_params=_params("arbitrary", "arbitrary", "arbitrary"),
        name="wkv7_scan",
    )(wm, uv, y0, arb, rt, bt, kt, v, gate, pl_arr,
      r_k.reshape(1, d), ln_w.reshape(1, d), ln_b.reshape(1, d), bd)


def _tile(n, want):
    return min(n, want)


def kernel(x, norm_g, final_g, ffn_w_gu, ffn_w_d, conv_w_in, conv_w, conv_w_out, rwkv_mu, rwkv_w_rkv, rwkv_w0, rwkv_w1, rwkv_w2, rwkv_a0, rwkv_a1, rwkv_a2, rwkv_g1, rwkv_g2, rwkv_k_k, rwkv_k_a, rwkv_r_k, rwkv_ln_w, rwkv_ln_b, rwkv_w_o, rwkv_v0, rwkv_v1, rwkv_v2):
    b, s, d = x.shape
    depth = norm_g.shape[0]
    f = ffn_w_d.shape[2]
    assert d % (WKV_PREP_GROUPS * WKV_GROUP) == 0 and d % (WKV_SCAN_GROUPS * WKV_GROUP) == 0
    assert s % WKV_CHUNK == 0
    t = b * s
    ffn_tm = _tile(t, 1024)
    w_gu = _mx(ffn_w_gu)
    w_d = _mx(ffn_w_d)

    def ffn(x3, i, half, mix=None, final_norm=False):
        y2, w_o = (None, None) if mix is None else (mix[0].reshape(t, d), mix[1])
        y = _ffn(x3.reshape(t, d), norm_g[i, 2 * half], w_gu[i, half], w_d[i, half], final_g,
                 y2, w_o, final_norm=final_norm, tm=ffn_tm)
        return y.reshape(b, s, d)

    v_first = None
    for i in range(depth):
        x = ffn(x, i, 0)
        if i % 2 == 0:
            c = i // 2
            y = _conv_mixer(x, norm_g[i, 1], _mx(conv_w_in[c]), conv_w[c], tm=_tile(s, 512))
            w_o = _mx(conv_w_out[c])
        else:
            j = i // 2
            v_res = None if j == 0 else (rwkv_v0[j - 1], rwkv_v1[j - 1], rwkv_v2[j - 1])
            outs = _rwkv_proj(
                x, norm_g[i, 1], rwkv_mu[j], rwkv_w_rkv[j], rwkv_w0[j], rwkv_w1[j], rwkv_w2[j],
                rwkv_a0[j], rwkv_a1[j], rwkv_a2[j], rwkv_g1[j], rwkv_g2[j],
                rwkv_k_k[j], rwkv_k_a[j], v_res, v_first, tm=_tile(s, 256))
            rt, at, bt, kt, v, gate, pl_arr = outs[:7]
            if j == 0:
                v_first = outs[7]
            y = _wkv(rt, at, bt, kt, v, gate, pl_arr, rwkv_r_k[j].reshape(-1),
                     rwkv_ln_w[j], rwkv_ln_b[j])
            w_o = _mx(rwkv_w_o[j])
        x = ffn(x, i, 1, mix=(y, w_o), final_norm=(i == depth - 1))
    return x
```

```python
import functools
import math

import jax
import jax.numpy as jnp
from jax import lax
from jax.experimental import pallas as pl
from jax.experimental.pallas import tpu as pltpu

NORM_EPS = 1e-5
GN_EPS = 64e-5
HEAD_SIZE = 64
WKV_CHUNK = 64
WKV_GROUP = 256
WKV_PREP_GROUP = 128
WKV_PREP_BLOCK = 512
WKV_PREP_GROUPS = 4
WKV_SCAN_BLOCK = 256
WKV_SCAN_GROUPS = 4
WKV_SCAN_BATCH = 2
FFN_CHUNK = 512
SUBLANES = 8
LANES = 128
DECAY_SCALE = math.exp(-0.5)
VMEM_LIMIT = 56 * 1024 * 1024

_MXU_DTYPE = jnp.bfloat16


def _mx(x):
    return x.astype(_MXU_DTYPE)


def _dot(a, b):
    return jnp.dot(_mx(a), _mx(b), preferred_element_type=jnp.float32)


def _dot_nt(a, b):
    return lax.dot_general(_mx(a), _mx(b), (((1,), (1,)), ((), ())),
                           preferred_element_type=jnp.float32)


def _dot_tn(a, b):
    return lax.dot_general(_mx(a), _mx(b), (((0,), (0,)), ((), ())),
                           preferred_element_type=jnp.float32)


def _rms(x, g):
    ms = jnp.mean(x * x, axis=-1, keepdims=True)
    return x * lax.rsqrt(ms + NORM_EPS) * g


def _params(*sem):
    return pltpu.CompilerParams(dimension_semantics=sem, vmem_limit_bytes=VMEM_LIMIT)


def _ffn_kernel(*refs, mixer_out, final_norm, chunks):
    if mixer_out:
        x_ref, y_ref, wo_ref, g_ref, wgu_ref, wd_ref, fg_ref, o_ref, act_ref = refs
        x = x_ref[...] + jnp.dot(y_ref[...], wo_ref[...], preferred_element_type=jnp.float32)
    else:
        x_ref, g_ref, wgu_ref, wd_ref, fg_ref, o_ref, act_ref = refs
        x = x_ref[...]
    f = wd_ref.shape[0]
    h = _mx(_rms(x, g_ref[...]))
    for lo, hi in chunks:
        gate = jnp.dot(h, wgu_ref[:, lo:hi], preferred_element_type=jnp.float32)
        up = jnp.dot(h, wgu_ref[:, f + lo:f + hi], preferred_element_type=jnp.float32)
        act_ref[:, lo:hi] = _mx(gate * jax.nn.sigmoid(gate) * up)
    y = x + 0.5 * jnp.dot(act_ref[...], wd_ref[...], preferred_element_type=jnp.float32)
    if final_norm:
        y = _rms(y, fg_ref[...])
    o_ref[...] = y


def _resident(shape):
    return pl.BlockSpec(shape, lambda i: (0,) * len(shape), pipeline_mode=pl.Buffered(1))


def _ffn(x2, g, w_gu, w_d, final_g, y2=None, w_o=None, *, final_norm, tm):
    t, d = x2.shape
    f = w_d.shape[0]
    chunks = tuple((lo, min(lo + FFN_CHUNK, f)) for lo in range(0, f, FFN_CHUNK))
    tok = pl.BlockSpec((tm, d), lambda i: (i, 0))
    mixer_out = y2 is not None
    args, in_specs = [x2], [tok]
    if mixer_out:
        args += [y2, w_o]
        in_specs += [tok, _resident((d, d))]
    args += [g.reshape(1, d), w_gu, w_d, final_g.reshape(1, d)]
    in_specs += [_resident((1, d)), _resident((d, 2 * f)), _resident((f, d)), _resident((1, d))]
    return pl.pallas_call(
        functools.partial(_ffn_kernel, mixer_out=mixer_out, final_norm=final_norm, chunks=chunks),
        grid=(t // tm,),
        in_specs=in_specs,
        out_specs=tok,
        out_shape=jax.ShapeDtypeStruct((t, d), jnp.float32),
        scratch_shapes=[pltpu.VMEM((tm, f), _MXU_DTYPE)],
        compiler_params=_params("parallel"),
        name="ffn",
    )(*args)


def _conv_kernel(x_ref, g_ref, win_ref, cw_ref, o_ref, zbuf_ref):
    tm, d = x_ref.shape[1], x_ref.shape[2]

    @pl.when(pl.program_id(1) == 0)
    def _():
        zbuf_ref[0:SUBLANES, :] = jnp.zeros((SUBLANES, d), jnp.float32)

    h = _mx(_rms(x_ref[0], g_ref[...]))
    gb = jnp.dot(h, win_ref[:, 0:d], preferred_element_type=jnp.float32)
    gc = jnp.dot(h, win_ref[:, d:2 * d], preferred_element_type=jnp.float32)
    hh = jnp.dot(h, win_ref[:, 2 * d:3 * d], preferred_element_type=jnp.float32)
    z = gc * hh
    zbuf_ref[SUBLANES:SUBLANES + tm, :] = z
    z1 = zbuf_ref[SUBLANES - 1:SUBLANES - 1 + tm, :]
    z2 = zbuf_ref[SUBLANES - 2:SUBLANES - 2 + tm, :]
    conv = cw_ref[2:3, :] * z + cw_ref[1:2, :] * z1 + cw_ref[0:1, :] * z2
    zbuf_ref[0:SUBLANES, :] = z[tm - SUBLANES:tm, :]
    o_ref[0] = _mx(gb * conv)


def _conv_mixer(x, g, w_in, conv_w, *, tm):
    b, s, d = x.shape
    return pl.pallas_call(
        _conv_kernel,
        grid=(b, s // tm),
        in_specs=[
            pl.BlockSpec((1, tm, d), lambda i, j: (i, j, 0)),
            pl.BlockSpec((1, d), lambda i, j: (0, 0)),
            pl.BlockSpec((d, 3 * d), lambda i, j: (0, 0)),
            pl.BlockSpec(conv_w.shape, lambda i, j: (0, 0)),
        ],
        out_specs=pl.BlockSpec((1, tm, d), lambda i, j: (i, j, 0)),
        out_shape=jax.ShapeDtypeStruct((b, s, d), _MXU_DTYPE),
        scratch_shapes=[pltpu.VMEM((tm + SUBLANES, d), jnp.float32)],
        compiler_params=_params("arbitrary", "arbitrary"),
        name="conv_mixer",
    )(x, g.reshape(1, d), w_in, conv_w)


def _head_sums(x, bd):
    d = x.shape[1]
    parts = [_dot(x[:, c:c + WKV_GROUP], bd) for c in range(0, d, WKV_GROUP)]
    return jnp.concatenate(parts, axis=1)


def _split3(x):
    hi = _mx(x)
    r1 = x - hi.astype(jnp.float32)
    mid = _mx(r1)
    lo = _mx(r1 - mid.astype(jnp.float32))
    return hi, mid, lo


def _proj_kernel(*refs, value_residual):
    if value_residual:
        (x_ref, g_ref, mu_ref, wr_ref, wk_ref, wv_ref, w0_ref, w1_ref, w2_ref,
         a0_ref, a1_ref, a2_ref, g1_ref, g2_ref, kk_ref, ka_ref, tri_ref, bd_ref,
         v0_ref, v1_ref, v2_ref, vf_ref,
         rt_ref, at_ref, bt_ref, kt_ref, v_ref, gate_ref, pl_ref, hbuf_ref) = refs
    else:
        (x_ref, g_ref, mu_ref, wr_ref, wk_ref, wv_ref, w0_ref, w1_ref, w2_ref,
         a0_ref, a1_ref, a2_ref, g1_ref, g2_ref, kk_ref, ka_ref, tri_ref, bd_ref,
         rt_ref, at_ref, bt_ref, kt_ref, v_ref, gate_ref, pl_ref, vf_ref, hbuf_ref) = refs
    tm, d = x_ref.shape[1], x_ref.shape[2]

    @pl.when(pl.program_id(1) == 0)
    def _():
        hbuf_ref[0:SUBLANES, :] = jnp.zeros((SUBLANES, d), jnp.float32)

    h = _rms(x_ref[0], g_ref[...])
    hbuf_ref[SUBLANES:SUBLANES + tm, :] = h
    xx = hbuf_ref[SUBLANES - 1:SUBLANES - 1 + tm, :] - h
    hbuf_ref[0:SUBLANES, :] = h[tm - SUBLANES:tm, :]

    def mix(p):
        return _mx(h + xx * mu_ref[p:p + 1, :])

    r = jnp.dot(mix(0), wr_ref[...], preferred_element_type=jnp.float32)
    k = jnp.dot(mix(1), wk_ref[...], preferred_element_type=jnp.float32)
    xv = mix(2)
    v = jnp.dot(xv, wv_ref[...], preferred_element_type=jnp.float32)
    zw = w0_ref[...] + _dot(jnp.tanh(jnp.dot(mix(3), w1_ref[...],
                                             preferred_element_type=jnp.float32)), w2_ref[...])
    lw = -DECAY_SCALE * jax.nn.sigmoid(zw)
    asig = jax.nn.sigmoid(a0_ref[...] + _dot(jnp.dot(mix(4), a1_ref[...],
                                                     preferred_element_type=jnp.float32),
                                             a2_ref[...]))
    gate_ref[0] = _mx(_dot(jax.nn.sigmoid(jnp.dot(mix(5), g1_ref[...],
                                                  preferred_element_type=jnp.float32)),
                           g2_ref[...]))

    kk = k * kk_ref[...]
    kk = kk * lax.rsqrt(jnp.maximum(_head_sums(kk * kk, bd_ref[...]), 1e-24))
    kp = k * (1.0 + (asig - 1.0) * ka_ref[...])
    if value_residual:
        mixv = jax.nn.sigmoid(v0_ref[...] + _dot(jnp.dot(xv, v1_ref[...],
                                                         preferred_element_type=jnp.float32),
                                                 v2_ref[...]))
        v = v + (vf_ref[0] - v) * mixv
    else:
        vf_ref[0] = v
    v_ref[0] = _mx(v)

    tri = tri_ref[...]
    hi, mid, lo = _split3(lw)
    c = (jnp.dot(tri, hi, preferred_element_type=jnp.float32)
         + jnp.dot(tri, mid, preferred_element_type=jnp.float32)
         + jnp.dot(tri, lo, preferred_element_type=jnp.float32))
    pinc = jnp.exp(c)
    pinv = jnp.exp(-c)
    rt_ref[0] = _mx(r * pinc)
    at_ref[0] = _mx(-kk * jnp.exp(c - lw))
    bt_ref[0] = _mx(kk * asig * pinv)
    kt_ref[0] = _mx(kp * pinv)
    for ci in range(tm // WKV_CHUNK):
        row = ci * WKV_CHUNK + WKV_CHUNK - 1
        pl_ref[0, ci] = pinc[row:row + 1, :]


def _pad_cols(w, n):
    return jnp.pad(w, ((0, 0), (0, n - w.shape[1])))


def _pad_rows(w, n):
    return jnp.pad(w, ((0, n - w.shape[0]), (0, 0)))


def _lora_width(n):
    return -(-n // LANES) * LANES


def _rwkv_proj(x, g, mu, w_rkv, w0, w1, w2, a0, a1, a2, g1, g2, k_k, k_a, v_res, v_first, *, tm):
    b, s, d = x.shape
    row = lambda a: a.reshape(1, d)
    cst = lambda shape: pl.BlockSpec(shape, lambda i, j: (0,) * len(shape))
    tok = pl.BlockSpec((1, tm, d), lambda i, j: (i, j, 0))

    def lora(wa, wb):
        n = _lora_width(wa.shape[1])
        return _mx(_pad_cols(wa, n)), _mx(_pad_rows(wb, n))

    w1p, w2p = lora(w1, w2)
    a1p, a2p = lora(a1, a2)
    g1p, g2p = lora(g1, g2)
    idx = jnp.arange(tm)
    tri = ((idx[:, None] >= idx[None, :])
           & (idx[:, None] // WKV_CHUNK == idx[None, :] // WKV_CHUNK)).astype(_MXU_DTYPE)
    bd = _head_block_diag(WKV_GROUP)

    args = [x, row(g), mu, _mx(w_rkv[0]), _mx(w_rkv[1]), _mx(w_rkv[2]), row(w0), w1p, w2p,
            row(a0), a1p, a2p, g1p, g2p, row(k_k), row(k_a), tri, bd]
    in_specs = [tok] + [cst(a.shape) for a in args[1:]]
    if v_res is not None:
        v0, v1, v2 = v_res
        v1p, v2p = lora(v1, v2)
        extra = [row(v0), v1p, v2p]
        args += extra + [v_first]
        in_specs += [cst(a.shape) for a in extra] + [tok]

    nchunk = tm // WKV_CHUNK
    out_specs = [tok] * 6 + [pl.BlockSpec((1, nchunk, 1, d), lambda i, j: (i, j, 0, 0))]
    out_shape = ([jax.ShapeDtypeStruct((b, s, d), _MXU_DTYPE)] * 6
                 + [jax.ShapeDtypeStruct((b, s // WKV_CHUNK, 1, d), jnp.float32)])
    if v_res is None:
        out_specs.append(tok)
        out_shape.append(jax.ShapeDtypeStruct((b, s, d), jnp.float32))
    outs = pl.pallas_call(
        functools.partial(_proj_kernel, value_residual=v_res is not None),
        grid=(b, s // tm),
        in_specs=in_specs,
        out_specs=out_specs,
        out_shape=out_shape,
        scratch_shapes=[pltpu.VMEM((tm + SUBLANES, d), jnp.float32)],
        compiler_params=_params("arbitrary", "arbitrary"),
        name="rwkv_proj",
    )(*args)
    return outs


def _stack(x, bd):
    xb = _mx(x)
    return jnp.concatenate([xb] * (bd.shape[0] // xb.shape[0]), axis=0) * bd


def _wkv_prep_kernel(rt_ref, at_ref, bt_ref, kt_ref, v_ref, bd_ref, msk_ref,
                     wm_ref, uv_ref, y0_ref, arb_ref):
    L, W = WKV_CHUNK, WKV_PREP_GROUP
    bd = bd_ref[...]
    m_strict = msk_ref[0]
    m_incl = msk_ref[1]
    eye = msk_ref[2]
    nlev = msk_ref.shape[0] - 4
    items = [(c, g) for c in range(rt_ref.shape[1] // L) for g in range(rt_ref.shape[2] // W)]
    n = range(len(items))
    st = lambda x: _stack(x, bd)
    where = lambda i: (0, slice(items[i][0] * L, (items[i][0] + 1) * L),
                       slice(items[i][1] * W, (items[i][1] + 1) * W))

    rt = [rt_ref[where(i)] for i in n]
    at = [at_ref[where(i)] for i in n]
    vs = [st(v_ref[where(i)]) for i in n]
    bk = [jnp.concatenate([st(bt_ref[where(i)]), st(kt_ref[where(i)])], axis=0) for i in n]
    aa = [_dot_nt(jnp.concatenate([at[i], rt[i]], axis=0), bk[i]) for i in n]
    a_ab = [aa[i][:L, :W] for i in n]
    a_ak = [aa[i][:L, W:] * m_strict for i in n]
    a_rk = [aa[i][L:, W:] * m_incl for i in n]
    for i in n:
        arb_ref[where(i)] = _mx(aa[i][L:, :W] * m_incl)
    T = [eye + a_ab[i] * msk_ref[3] for i in n]
    for lev in range(nlev):
        w1 = [_dot(a_ab[i] * msk_ref[4 + lev], st(T[i])) for i in n]
        T = [T[i] + _dot(T[i], st(w1[i])) for i in n]
    avy = [_dot(jnp.concatenate([a_ak[i], a_rk[i]], axis=0), vs[i]) for i in n]
    wu = [_dot(T[i], jnp.concatenate([st(at[i]), st(avy[i][:L])], axis=1)) for i in n]
    for i in n:
        wm_ref[where(i)] = _mx(wu[i][:, :W])
        uv_ref[where(i)] = _mx(wu[i][:, W:])
        y0_ref[where(i)] = _mx(avy[i][L:])


def _wkv_scan_kernel(wm_ref, uv_ref, y0_ref, arb_ref, rt_ref, bt_ref, kt_ref, v_ref, gate_ref,
                     pl_ref, rk_ref, lnw_ref, lnb_ref, bd_ref, o_ref, g_ref):
    L, W = WKV_CHUNK, WKV_GROUP
    nchunk = rt_ref.shape[1] // L
    chains = [(i, g) for i in range(rt_ref.shape[0]) for g in range(rt_ref.shape[2] // W)]

    @pl.when(pl.program_id(2) == 0)
    def _():
        g_ref[...] = jnp.zeros_like(g_ref)

    bd = bd_ref[...]
    st = lambda x: _stack(x, bd)
    G = [g_ref[n] for n in range(len(chains))]
    ys = [[] for _ in chains]
    for c in range(nchunk):
        for n, (i, g) in enumerate(chains):
            lanes = slice(g * W, (g + 1) * W)
            at_ = (i, slice(c * L, (c + 1) * L), lanes)
            bk = jnp.concatenate([st(bt_ref[at_]), st(kt_ref[at_])], axis=0)
            ur = _dot_nt(jnp.concatenate([wm_ref[at_], rt_ref[at_]], axis=0), G[n])
            us = st(ur[:L] + uv_ref[at_])
            ys[n].append(ur[L:] + _dot(arb_ref[at_], us) + y0_ref[at_])
            upd = _dot_tn(jnp.concatenate([us, st(v_ref[at_])], axis=0), bk)
            G[n] = (G[n] + upd) * pl_ref[i, c][:, lanes]
    for n in range(len(chains)):
        g_ref[n] = G[n]

    for n, (i, g) in enumerate(chains):
        lanes = slice(g * W, (g + 1) * W)
        y = jnp.concatenate(ys[n], axis=0)
        mean = _dot(y, bd) * (1.0 / HEAD_SIZE)
        yc = y - mean
        var = _dot(yc * yc, bd) * (1.0 / HEAD_SIZE)
        yn = yc * lax.rsqrt(var + GN_EPS) * lnw_ref[:, lanes] + lnb_ref[:, lanes]
        rkk = rt_ref[i, :, lanes].astype(jnp.float32) * kt_ref[i, :, lanes] * rk_ref[:, lanes]
        bonus = _dot(rkk, bd) * v_ref[i, :, lanes]
        o_ref[i, :, lanes] = ((yn + bonus) * gate_ref[i, :, lanes]).astype(o_ref.dtype)


def _wkv_masks(lanes):
    L = WKV_CHUNK
    t = jnp.arange(L)[:, None]
    s = (jnp.arange(lanes) % L)[None, :]
    masks = [s < t, s <= t, s == t]
    n = 1
    while n < L:
        masks.append((s < t) & (t // (2 * n) == s // (2 * n)) & (t // n != s // n))
        n *= 2
    return jnp.stack(masks).astype(jnp.float32)


def _head_block_diag(lanes):
    head = jnp.arange(lanes) // HEAD_SIZE
    return (head[:, None] == head[None, :]).astype(_MXU_DTYPE)


def _wkv(rt, at, bt, kt, v, gate, pl_arr, r_k, ln_w, ln_b):
    b, s, d = rt.shape
    assert WKV_CHUNK == HEAD_SIZE
    bd = _head_block_diag(WKV_PREP_GROUP)
    masks = _wkv_masks(WKV_PREP_GROUP)

    blk = min(WKV_PREP_BLOCK, s)
    width = WKV_PREP_GROUPS * WKV_PREP_GROUP
    tok = pl.BlockSpec((1, blk, width), lambda i, j, k: (i, k, j))
    shape = lambda dt: jax.ShapeDtypeStruct((b, s, d), dt)
    wm, uv, y0, arb = pl.pallas_call(
        _wkv_prep_kernel,
        grid=(b, d // width, s // blk),
        in_specs=[tok] * 5 + [pl.BlockSpec(bd.shape, lambda i, j, k: (0, 0)),
                              pl.BlockSpec(masks.shape, lambda i, j, k: (0, 0, 0))],
        out_specs=[tok] * 4,
        out_shape=[shape(_MXU_DTYPE)] * 4,
        compiler_params=_params("parallel", "parallel", "parallel"),
        name="wkv7_prep",
    )(rt, at, bt, kt, v, bd, masks)

    bd = _head_block_diag(WKV_GROUP)
    blk = min(WKV_SCAN_BLOCK, s)
    width = WKV_SCAN_GROUPS * WKV_GROUP
    nb = WKV_SCAN_BATCH if b % WKV_SCAN_BATCH == 0 else 1
    tok = pl.BlockSpec((nb, blk, width), lambda i, j, k: (i, k, j))
    par = pl.BlockSpec((1, width), lambda i, j, k: (0, j))
    return pl.pallas_call(
        _wkv_scan_kernel,
        grid=(b // nb, d // width, s // blk),
        in_specs=[tok] * 9 + [
            pl.BlockSpec((nb, blk // WKV_CHUNK, 1, width), lambda i, j, k: (i, k, 0, j)),
            par, par, par,
            pl.BlockSpec(bd.shape, lambda i, j, k: (0, 0)),
        ],
        out_specs=tok,
        out_shape=jax.ShapeDtypeStruct((b, s, d), _MXU_DTYPE),
        scratch_shapes=[pltpu.VMEM((nb * WKV_SCAN_GROUPS, WKV_GROUP, WKV_GROUP), jnp.float32)],
        compiler_params=_params("arbitrary", "arbitrary", "arbitrary"),
        name="wkv7_scan",
    )(wm, uv, y0, arb, rt, bt, kt, v, gate, pl_arr,
      r_k.reshape(1, d), ln_w.reshape(1, d), ln_b.reshape(1, d), bd)


def _tile(n, want):
    return min(n, want)


def kernel(x, norm_g, final_g, ffn_w_gu, ffn_w_d, conv_w_in, conv_w, conv_w_out, rwkv_mu, rwkv_w_rkv, rwkv_w0, rwkv_w1, rwkv_w2, rwkv_a0, rwkv_a1, rwkv_a2, rwkv_g1, rwkv_g2, rwkv_k_k, rwkv_k_a, rwkv_r_k, rwkv_ln_w, rwkv_ln_b, rwkv_w_o, rwkv_v0, rwkv_v1, rwkv_v2):
    b, s, d = x.shape
    depth = norm_g.shape[0]
    f = ffn_w_d.shape[2]
    assert d % (WKV_PREP_GROUPS * WKV_PREP_GROUP) == 0 and d % (WKV_SCAN_GROUPS * WKV_GROUP) == 0
    assert s % WKV_CHUNK == 0
    t = b * s
    ffn_tm = _tile(t, 1024)
    w_gu = _mx(ffn_w_gu)
    w_d = _mx(ffn_w_d)

    def ffn(x3, i, half, mix=None, final_norm=False):
        y2, w_o = (None, None) if mix is None else (mix[0].reshape(t, d), mix[1])
        y = _ffn(x3.reshape(t, d), norm_g[i, 2 * half], w_gu[i, half], w_d[i, half], final_g,
                 y2, w_o, final_norm=final_norm, tm=ffn_tm)
        return y.reshape(b, s, d)

    v_first = None
    for i in range(depth):
        x = ffn(x, i, 0)
        if i % 2 == 0:
            c = i // 2
            y = _conv_mixer(x, norm_g[i, 1], _mx(conv_w_in[c]), conv_w[c], tm=_tile(s, 512))
            w_o = _mx(conv_w_out[c])
        else:
            j = i // 2
            v_res = None if j == 0 else (rwkv_v0[j - 1], rwkv_v1[j - 1], rwkv_v2[j - 1])
            outs = _rwkv_proj(
                x, norm_g[i, 1], rwkv_mu[j], rwkv_w_rkv[j], rwkv_w0[j], rwkv_w1[j], rwkv_w2[j],
                rwkv_a0[j], rwkv_a1[j], rwkv_a2[j], rwkv_g1[j], rwkv_g2[j],
                rwkv_k_k[j], rwkv_k_a[j], v_res, v_first, tm=_tile(s, 256))
            rt, at, bt, kt, v, gate, pl_arr = outs[:7]
            if j == 0:
                v_first = outs[7]
            y = _wkv(rt, at, bt, kt, v, gate, pl_arr, rwkv_r_k[j].reshape(-1),
                     rwkv_ln_w[j], rwkv_ln_b[j])
            w_o = _mx(rwkv_w_o[j])
        x = ffn(x, i, 1, mix=(y, w_o), final_norm=(i == depth - 1))
    return x
```

```python
import functools
import math

import jax
import jax.numpy as jnp
from jax import lax
from jax.experimental import pallas as pl
from jax.experimental.pallas import tpu as pltpu

NORM_EPS = 1e-5
GN_EPS = 64e-5
HEAD_SIZE = 64
WKV_CHUNK = 64
WKV_GROUP = 256
WKV_PREP_GROUP = 128
WKV_BLOCK = 256
WKV_BATCH = 2
WKV_PREP_STAGES_PER_SCAN_STAGE = 2
FFN_CHUNK = 512
SUBLANES = 8
LANES = 128
DECAY_SCALE = math.exp(-0.5)
VMEM_LIMIT = 56 * 1024 * 1024

_MXU_DTYPE = jnp.bfloat16
_DONE = object()


def _mx(x):
    return x.astype(_MXU_DTYPE)


def _dot(a, b):
    return jnp.dot(_mx(a), _mx(b), preferred_element_type=jnp.float32)


def _dot_nt(a, b):
    return lax.dot_general(_mx(a), _mx(b), (((1,), (1,)), ((), ())),
                           preferred_element_type=jnp.float32)


def _dot_tn(a, b):
    return lax.dot_general(_mx(a), _mx(b), (((0,), (0,)), ((), ())),
                           preferred_element_type=jnp.float32)


def _rms(x, g):
    ms = jnp.mean(x * x, axis=-1, keepdims=True)
    return x * lax.rsqrt(ms + NORM_EPS) * g


def _params(*sem):
    return pltpu.CompilerParams(dimension_semantics=sem, vmem_limit_bytes=VMEM_LIMIT)


def _ffn_kernel(*refs, mixer_out, final_norm, chunks):
    if mixer_out:
        x_ref, y_ref, wo_ref, g_ref, wgu_ref, wd_ref, fg_ref, o_ref, act_ref = refs
        x = x_ref[...] + jnp.dot(y_ref[...], wo_ref[...], preferred_element_type=jnp.float32)
    else:
        x_ref, g_ref, wgu_ref, wd_ref, fg_ref, o_ref, act_ref = refs
        x = x_ref[...]
    f = wd_ref.shape[0]
    h = _mx(_rms(x, g_ref[...]))
    for lo, hi in chunks:
        gate = jnp.dot(h, wgu_ref[:, lo:hi], preferred_element_type=jnp.float32)
        up = jnp.dot(h, wgu_ref[:, f + lo:f + hi], preferred_element_type=jnp.float32)
        act_ref[:, lo:hi] = _mx(gate * jax.nn.sigmoid(gate) * up)
    y = x + 0.5 * jnp.dot(act_ref[...], wd_ref[...], preferred_element_type=jnp.float32)
    if final_norm:
        y = _rms(y, fg_ref[...])
    o_ref[...] = y


def _resident(shape):
    return pl.BlockSpec(shape, lambda i: (0,) * len(shape), pipeline_mode=pl.Buffered(1))


def _ffn(x2, g, w_gu, w_d, final_g, y2=None, w_o=None, *, final_norm, tm):
    t, d = x2.shape
    f = w_d.shape[0]
    chunks = tuple((lo, min(lo + FFN_CHUNK, f)) for lo in range(0, f, FFN_CHUNK))
    tok = pl.BlockSpec((tm, d), lambda i: (i, 0))
    mixer_out = y2 is not None
    args, in_specs = [x2], [tok]
    if mixer_out:
        args += [y2, w_o]
        in_specs += [tok, _resident((d, d))]
    args += [g.reshape(1, d), w_gu, w_d, final_g.reshape(1, d)]
    in_specs += [_resident((1, d)), _resident((d, 2 * f)), _resident((f, d)), _resident((1, d))]
    return pl.pallas_call(
        functools.partial(_ffn_kernel, mixer_out=mixer_out, final_norm=final_norm, chunks=chunks),
        grid=(t // tm,),
        in_specs=in_specs,
        out_specs=tok,
        out_shape=jax.ShapeDtypeStruct((t, d), jnp.float32),
        scratch_shapes=[pltpu.VMEM((tm, f), _MXU_DTYPE)],
        compiler_params=_params("parallel"),
        name="ffn",
    )(*args)


def _conv_kernel(x_ref, g_ref, win_ref, cw_ref, o_ref, zbuf_ref):
    tm, d = x_ref.shape[1], x_ref.shape[2]

    @pl.when(pl.program_id(1) == 0)
    def _():
        zbuf_ref[0:SUBLANES, :] = jnp.zeros((SUBLANES, d), jnp.float32)

    h = _mx(_rms(x_ref[0], g_ref[...]))
    gb = jnp.dot(h, win_ref[:, 0:d], preferred_element_type=jnp.float32)
    gc = jnp.dot(h, win_ref[:, d:2 * d], preferred_element_type=jnp.float32)
    hh = jnp.dot(h, win_ref[:, 2 * d:3 * d], preferred_element_type=jnp.float32)
    z = gc * hh
    zbuf_ref[SUBLANES:SUBLANES + tm, :] = z
    z1 = zbuf_ref[SUBLANES - 1:SUBLANES - 1 + tm, :]
    z2 = zbuf_ref[SUBLANES - 2:SUBLANES - 2 + tm, :]
    conv = cw_ref[2:3, :] * z + cw_ref[1:2, :] * z1 + cw_ref[0:1, :] * z2
    zbuf_ref[0:SUBLANES, :] = z[tm - SUBLANES:tm, :]
    o_ref[0] = _mx(gb * conv)


def _conv_mixer(x, g, w_in, conv_w, *, tm):
    b, s, d = x.shape
    return pl.pallas_call(
        _conv_kernel,
        grid=(b, s // tm),
        in_specs=[
            pl.BlockSpec((1, tm, d), lambda i, j: (i, j, 0)),
            pl.BlockSpec((1, d), lambda i, j: (0, 0)),
            pl.BlockSpec((d, 3 * d), lambda i, j: (0, 0)),
            pl.BlockSpec(conv_w.shape, lambda i, j: (0, 0)),
        ],
        out_specs=pl.BlockSpec((1, tm, d), lambda i, j: (i, j, 0)),
        out_shape=jax.ShapeDtypeStruct((b, s, d), _MXU_DTYPE),
        scratch_shapes=[pltpu.VMEM((tm + SUBLANES, d), jnp.float32)],
        compiler_params=_params("arbitrary", "arbitrary"),
        name="conv_mixer",
    )(x, g.reshape(1, d), w_in, conv_w)


def _head_sums(x, bd):
    d = x.shape[1]
    parts = [_dot(x[:, c:c + WKV_GROUP], bd) for c in range(0, d, WKV_GROUP)]
    return jnp.concatenate(parts, axis=1)


def _split3(x):
    hi = _mx(x)
    r1 = x - hi.astype(jnp.float32)
    mid = _mx(r1)
    lo = _mx(r1 - mid.astype(jnp.float32))
    return hi, mid, lo


def _proj_kernel(*refs, value_residual):
    if value_residual:
        (x_ref, g_ref, mu_ref, wr_ref, wk_ref, wv_ref, w0_ref, w1_ref, w2_ref,
         a0_ref, a1_ref, a2_ref, g1_ref, g2_ref, kk_ref, ka_ref, tri_ref, bd_ref,
         v0_ref, v1_ref, v2_ref, vf_ref,
         rt_ref, at_ref, bt_ref, kt_ref, v_ref, gate_ref, pl_ref, hbuf_ref) = refs
    else:
        (x_ref, g_ref, mu_ref, wr_ref, wk_ref, wv_ref, w0_ref, w1_ref, w2_ref,
         a0_ref, a1_ref, a2_ref, g1_ref, g2_ref, kk_ref, ka_ref, tri_ref, bd_ref,
         rt_ref, at_ref, bt_ref, kt_ref, v_ref, gate_ref, pl_ref, vf_ref, hbuf_ref) = refs
    tm, d = x_ref.shape[1], x_ref.shape[2]

    @pl.when(pl.program_id(1) == 0)
    def _():
        hbuf_ref[0:SUBLANES, :] = jnp.zeros((SUBLANES, d), jnp.float32)

    h = _rms(x_ref[0], g_ref[...])
    hbuf_ref[SUBLANES:SUBLANES + tm, :] = h
    xx = hbuf_ref[SUBLANES - 1:SUBLANES - 1 + tm, :] - h
    hbuf_ref[0:SUBLANES, :] = h[tm - SUBLANES:tm, :]

    def mix(p):
        return _mx(h + xx * mu_ref[p:p + 1, :])

    r = jnp.dot(mix(0), wr_ref[...], preferred_element_type=jnp.float32)
    k = jnp.dot(mix(1), wk_ref[...], preferred_element_type=jnp.float32)
    xv = mix(2)
    v = jnp.dot(xv, wv_ref[...], preferred_element_type=jnp.float32)
    zw = w0_ref[...] + _dot(jnp.tanh(jnp.dot(mix(3), w1_ref[...],
                                             preferred_element_type=jnp.float32)), w2_ref[...])
    lw = -DECAY_SCALE * jax.nn.sigmoid(zw)
    asig = jax.nn.sigmoid(a0_ref[...] + _dot(jnp.dot(mix(4), a1_ref[...],
                                                     preferred_element_type=jnp.float32),
                                             a2_ref[...]))
    gate_ref[0] = _mx(_dot(jax.nn.sigmoid(jnp.dot(mix(5), g1_ref[...],
                                                  preferred_element_type=jnp.float32)),
                           g2_ref[...]))

    kk = k * kk_ref[...]
    kk = kk * lax.rsqrt(jnp.maximum(_head_sums(kk * kk, bd_ref[...]), 1e-24))
    kp = k * (1.0 + (asig - 1.0) * ka_ref[...])
    if value_residual:
        mixv = jax.nn.sigmoid(v0_ref[...] + _dot(jnp.dot(xv, v1_ref[...],
                                                         preferred_element_type=jnp.float32),
                                                 v2_ref[...]))
        v = v + (vf_ref[0] - v) * mixv
    else:
        vf_ref[0] = v
    v_ref[0] = _mx(v)

    tri = tri_ref[...]
    hi, mid, lo = _split3(lw)
    c = (jnp.dot(tri, hi, preferred_element_type=jnp.float32)
         + jnp.dot(tri, mid, preferred_element_type=jnp.float32)
         + jnp.dot(tri, lo, preferred_element_type=jnp.float32))
    pinc = jnp.exp(c)
    pinv = jnp.exp(-c)
    rt_ref[0] = _mx(r * pinc)
    at_ref[0] = _mx(-kk * jnp.exp(c - lw))
    bt_ref[0] = _mx(kk * asig * pinv)
    kt_ref[0] = _mx(kp * pinv)
    for ci in range(tm // WKV_CHUNK):
        row = ci * WKV_CHUNK + WKV_CHUNK - 1
        pl_ref[0, ci] = pinc[row:row + 1, :]


def _pad_cols(w, n):
    return jnp.pad(w, ((0, 0), (0, n - w.shape[1])))


def _pad_rows(w, n):
    return jnp.pad(w, ((0, n - w.shape[0]), (0, 0)))


def _lora_width(n):
    return -(-n // LANES) * LANES


def _rwkv_proj(x, g, mu, w_rkv, w0, w1, w2, a0, a1, a2, g1, g2, k_k, k_a, v_res, v_first, *, tm):
    b, s, d = x.shape
    row = lambda a: a.reshape(1, d)
    cst = lambda shape: pl.BlockSpec(shape, lambda i, j: (0,) * len(shape))
    tok = pl.BlockSpec((1, tm, d), lambda i, j: (i, j, 0))

    def lora(wa, wb):
        n = _lora_width(wa.shape[1])
        return _mx(_pad_cols(wa, n)), _mx(_pad_rows(wb, n))

    w1p, w2p = lora(w1, w2)
    a1p, a2p = lora(a1, a2)
    g1p, g2p = lora(g1, g2)
    idx = jnp.arange(tm)
    tri = ((idx[:, None] >= idx[None, :])
           & (idx[:, None] // WKV_CHUNK == idx[None, :] // WKV_CHUNK)).astype(_MXU_DTYPE)
    bd = _head_block_diag(WKV_GROUP)

    args = [x, row(g), mu, _mx(w_rkv[0]), _mx(w_rkv[1]), _mx(w_rkv[2]), row(w0), w1p, w2p,
            row(a0), a1p, a2p, g1p, g2p, row(k_k), row(k_a), tri, bd]
    in_specs = [tok] + [cst(a.shape) for a in args[1:]]
    if v_res is not None:
        v0, v1, v2 = v_res
        v1p, v2p = lora(v1, v2)
        extra = [row(v0), v1p, v2p]
        args += extra + [v_first]
        in_specs += [cst(a.shape) for a in extra] + [tok]

    nchunk = tm // WKV_CHUNK
    out_specs = [tok] * 6 + [pl.BlockSpec((1, nchunk, 1, d), lambda i, j: (i, j, 0, 0))]
    out_shape = ([jax.ShapeDtypeStruct((b, s, d), _MXU_DTYPE)] * 6
                 + [jax.ShapeDtypeStruct((b, s // WKV_CHUNK, 1, d), jnp.float32)])
    if v_res is None:
        out_specs.append(tok)
        out_shape.append(jax.ShapeDtypeStruct((b, s, d), jnp.float32))
    outs = pl.pallas_call(
        functools.partial(_proj_kernel, value_residual=v_res is not None),
        grid=(b, s // tm),
        in_specs=in_specs,
        out_specs=out_specs,
        out_shape=out_shape,
        scratch_shapes=[pltpu.VMEM((tm + SUBLANES, d), jnp.float32)],
        compiler_params=_params("arbitrary", "arbitrary"),
        name="rwkv_proj",
    )(*args)
    return outs


def _stack(x, bd):
    xb = _mx(x)
    return jnp.concatenate([xb] * (bd.shape[0] // xb.shape[0]), axis=0) * bd


def _wkv_prep(rt_ref, at_ref, bt_ref, kt_ref, v_ref, bd, msk_ref, results):
    L, W = WKV_CHUNK, WKV_PREP_GROUP
    m_strict = msk_ref[0]
    m_incl = msk_ref[1]
    eye = msk_ref[2]
    nlev = msk_ref.shape[0] - 4
    nb, rows, lanes = rt_ref.shape
    st = lambda x: _stack(x, bd)
    for b in range(nb):
        where = [(b, slice(c * L, (c + 1) * L), slice(g * W, (g + 1) * W))
                 for c in range(rows // L) for g in range(lanes // W)]
        n = range(len(where))
        rt = [rt_ref[w] for w in where]
        at = [at_ref[w] for w in where]
        vs = [st(v_ref[w]) for w in where]
        bk = [jnp.concatenate([st(bt_ref[w]), st(kt_ref[w])], axis=0) for w in where]
        yield
        aa = [_dot_nt(jnp.concatenate([at[i], rt[i]], axis=0), bk[i]) for i in n]
        a_ab = [aa[i][:L, :W] for i in n]
        a_ak = [aa[i][:L, W:] * m_strict for i in n]
        a_rk = [aa[i][L:, W:] * m_incl for i in n]
        a_rb = [_mx(aa[i][L:, :W] * m_incl) for i in n]
        yield
        T = [eye + a_ab[i] * msk_ref[3] for i in n]
        for lev in range(nlev):
            w1 = [_dot(a_ab[i] * msk_ref[4 + lev], st(T[i])) for i in n]
            yield
            T = [T[i] + _dot(T[i], st(w1[i])) for i in n]
            yield
        avy = [_dot(jnp.concatenate([a_ak[i], a_rk[i]], axis=0), vs[i]) for i in n]
        yield
        wu = [_dot(T[i], jnp.concatenate([st(at[i]), st(avy[i][:L])], axis=1)) for i in n]
        yield
        for i, w in enumerate(where):
            results.append((w, _mx(wu[i][:, :W]), _mx(wu[i][:, W:]), _mx(avy[i][L:]), a_rb[i]))
        yield


def _wkv_scan(wm_ref, uv_ref, y0_ref, arb_ref, rt_ref, bt_ref, kt_ref, v_ref, gate_ref,
              pl_ref, rk_ref, lnw_ref, lnb_ref, bd, o_ref, g_ref):
    L, W = WKV_CHUNK, WKV_GROUP
    nchunk = rt_ref.shape[1] // L
    chains = [(i, slice(g * W, (g + 1) * W))
              for i in range(rt_ref.shape[0]) for g in range(rt_ref.shape[2] // W)]
    nc = range(len(chains))
    st = lambda x: _stack(x, bd)
    G = [g_ref[n] for n in nc]
    ys = [[] for _ in chains]
    for c in range(nchunk):
        at_ = [(i, slice(c * L, (c + 1) * L), lanes) for i, lanes in chains]
        ur = [_dot_nt(jnp.concatenate([wm_ref[at_[n]], rt_ref[at_[n]]], axis=0), G[n]) for n in nc]
        bk = [jnp.concatenate([st(bt_ref[at_[n]]), st(kt_ref[at_[n]])], axis=0) for n in nc]
        vs = [st(v_ref[at_[n]]) for n in nc]
        yield
        us = [st(ur[n][:L] + uv_ref[at_[n]]) for n in nc]
        upd = [_dot_tn(jnp.concatenate([us[n], vs[n]], axis=0), bk[n]) for n in nc]
        for n in nc:
            ys[n].append(ur[n][L:] + _dot(arb_ref[at_[n]], us[n]) + y0_ref[at_[n]])
        yield
        G = [(G[n] + upd[n]) * pl_ref[chains[n][0], c][:, chains[n][1]] for n in nc]
    for n in nc:
        g_ref[n] = G[n]

    for n, (i, lanes) in enumerate(chains):
        y = jnp.concatenate(ys[n], axis=0)
        mean = _dot(y, bd) * (1.0 / HEAD_SIZE)
        yc = y - mean
        var = _dot(yc * yc, bd) * (1.0 / HEAD_SIZE)
        yn = yc * lax.rsqrt(var + GN_EPS) * lnw_ref[:, lanes] + lnb_ref[:, lanes]
        rkk = rt_ref[i, :, lanes].astype(jnp.float32) * kt_ref[i, :, lanes] * rk_ref[:, lanes]
        bonus = _dot(rkk, bd) * v_ref[i, :, lanes]
        o_ref[i, :, lanes] = ((yn + bonus) * gate_ref[i, :, lanes]).astype(o_ref.dtype)
        yield


def _wkv_kernel(rtp_ref, atp_ref, btp_ref, ktp_ref, vp_ref,
                rt_ref, bt_ref, kt_ref, v_ref, gate_ref, pl_ref,
                rk_ref, lnw_ref, lnb_ref, bdp_ref, msk_ref, bd_ref,
                o_ref, wm_ref, uv_ref, y0_ref, arb_ref, g_ref):
    k = pl.program_id(1)

    @pl.when(k == 0)
    def _():
        for ref in (wm_ref, uv_ref, y0_ref, arb_ref):
            ref[...] = jnp.zeros_like(ref)

    @pl.when(k <= 1)
    def _():
        g_ref[...] = jnp.zeros_like(g_ref)

    results = []
    scan = _wkv_scan(wm_ref, uv_ref, y0_ref, arb_ref, rt_ref, bt_ref, kt_ref, v_ref, gate_ref,
                     pl_ref, rk_ref, lnw_ref, lnb_ref, bd_ref[...], o_ref, g_ref)
    prep = _wkv_prep(rtp_ref, atp_ref, btp_ref, ktp_ref, vp_ref, bdp_ref[...], msk_ref, results)
    scan_live = prep_live = True
    while scan_live or prep_live:
        if scan_live:
            scan_live = next(scan, _DONE) is not _DONE
        for _ in range(WKV_PREP_STAGES_PER_SCAN_STAGE):
            if prep_live:
                prep_live = next(prep, _DONE) is not _DONE
    for w, wm, uv, y0, arb in results:
        wm_ref[w], uv_ref[w], y0_ref[w], arb_ref[w] = wm, uv, y0, arb


def _wkv_masks(lanes):
    L = WKV_CHUNK
    t = jnp.arange(L)[:, None]
    s = (jnp.arange(lanes) % L)[None, :]
    masks = [s < t, s <= t, s == t]
    n = 1
    while n < L:
        masks.append((s < t) & (t // (2 * n) == s // (2 * n)) & (t // n != s // n))
        n *= 2
    return jnp.stack(masks).astype(jnp.float32)


def _head_block_diag(lanes):
    head = jnp.arange(lanes) // HEAD_SIZE
    return (head[:, None] == head[None, :]).astype(_MXU_DTYPE)


def _wkv(rt, at, bt, kt, v, gate, pl_arr, r_k, ln_w, ln_b):
    b, s, d = rt.shape
    assert WKV_CHUNK == HEAD_SIZE
    bd_prep = _head_block_diag(WKV_PREP_GROUP)
    masks = _wkv_masks(WKV_PREP_GROUP)
    bd = _head_block_diag(WKV_GROUP)
    blk = min(WKV_BLOCK, s)
    nblk = s // blk
    nb = WKV_BATCH if b % WKV_BATCH == 0 else 1
    cur = pl.BlockSpec((nb, blk, d), lambda i, k: (i, jnp.minimum(k, nblk - 1), 0))
    prev = pl.BlockSpec((nb, blk, d), lambda i, k: (i, jnp.maximum(k - 1, 0), 0))
    cst = lambda a: pl.BlockSpec(a.shape, lambda i, k: (0,) * a.ndim)
    rows = [a.reshape(1, d) for a in (r_k, ln_w, ln_b)]
    consts = rows + [bd_prep, masks, bd]
    return pl.pallas_call(
        _wkv_kernel,
        grid=(b // nb, nblk + 1),
        in_specs=[cur] * 5 + [prev] * 5 + [
            pl.BlockSpec((nb, blk // WKV_CHUNK, 1, d), lambda i, k: (i, jnp.maximum(k - 1, 0), 0, 0)),
        ] + [cst(a) for a in consts],
        out_specs=prev,
        out_shape=jax.ShapeDtypeStruct((b, s, d), _MXU_DTYPE),
        scratch_shapes=[pltpu.VMEM((nb, blk, d), _MXU_DTYPE)] * 4
        + [pltpu.VMEM((nb * (d // WKV_GROUP), WKV_GROUP, WKV_GROUP), jnp.float32)],
        compiler_params=_params("arbitrary", "arbitrary"),
        name="wkv7",
    )(rt, at, bt, kt, v, rt, bt, kt, v, gate, pl_arr, *consts)


def _tile(n, want):
    return min(n, want)


def kernel(x, norm_g, final_g, ffn_w_gu, ffn_w_d, conv_w_in, conv_w, conv_w_out, rwkv_mu, rwkv_w_rkv, rwkv_w0, rwkv_w1, rwkv_w2, rwkv_a0, rwkv_a1, rwkv_a2, rwkv_g1, rwkv_g2, rwkv_k_k, rwkv_k_a, rwkv_r_k, rwkv_ln_w, rwkv_ln_b, rwkv_w_o, rwkv_v0, rwkv_v1, rwkv_v2):
    b, s, d = x.shape
    depth = norm_g.shape[0]
    f = ffn_w_d.shape[2]
    assert d % WKV_GROUP == 0 and WKV_GROUP % WKV_PREP_GROUP == 0
    assert s % WKV_CHUNK == 0
    t = b * s
    ffn_tm = _tile(t, 1024)
    w_gu = _mx(ffn_w_gu)
    w_d = _mx(ffn_w_d)

    def ffn(x3, i, half, mix=None, final_norm=False):
        y2, w_o = (None, None) if mix is None else (mix[0].reshape(t, d), mix[1])
        y = _ffn(x3.reshape(t, d), norm_g[i, 2 * half], w_gu[i, half], w_d[i, half], final_g,
                 y2, w_o, final_norm=final_norm, tm=ffn_tm)
        return y.reshape(b, s, d)

    v_first = None
    for i in range(depth):
        x = ffn(x, i, 0)
        if i % 2 == 0:
            c = i // 2
            y = _conv_mixer(x, norm_g[i, 1], _mx(conv_w_in[c]), conv_w[c], tm=_tile(s, 512))
            w_o = _mx(conv_w_out[c])
        else:
            j = i // 2
            v_res = None if j == 0 else (rwkv_v0[j - 1], rwkv_v1[j - 1], rwkv_v2[j - 1])
            outs = _rwkv_proj(
                x, norm_g[i, 1], rwkv_mu[j], rwkv_w_rkv[j], rwkv_w0[j], rwkv_w1[j], rwkv_w2[j],
                rwkv_a0[j], rwkv_a1[j], rwkv_a2[j], rwkv_g1[j], rwkv_g2[j],
                rwkv_k_k[j], rwkv_k_a[j], v_res, v_first, tm=_tile(s, 256))
            rt, at, bt, kt, v, gate, pl_arr = outs[:7]
            if j == 0:
                v_first = outs[7]
            y = _wkv(rt, at, bt, kt, v, gate, pl_arr, rwkv_r_k[j].reshape(-1),
                     rwkv_ln_w[j], rwkv_ln_b[j])
            w_o = _mx(rwkv_w_o[j])
        x = ffn(x, i, 1, mix=(y, w_o), final_norm=(i == depth - 1))
    return x
```

```python
import functools
import math

import jax
import jax.numpy as jnp
from jax import lax
from jax.experimental import pallas as pl
from jax.experimental.pallas import tpu as pltpu

NORM_EPS = 1e-5
GN_EPS = 64e-5
HEAD_SIZE = 64
WKV_CHUNK = 64
WKV_GROUP = 256
WKV_PREP_GROUP = 128
WKV_BLOCK = 256
WKV_BATCH = 2
WKV_PREP_STAGES_PER_SCAN_STAGE = 2
FFN_CHUNK = 512
SUBLANES = 8
LANES = 128
DECAY_SCALE = math.exp(-0.5)
VMEM_LIMIT = 56 * 1024 * 1024

_MXU_DTYPE = jnp.bfloat16
_DONE = object()


def _mx(x):
    return x.astype(_MXU_DTYPE)


def _dot(a, b):
    return jnp.dot(_mx(a), _mx(b), preferred_element_type=jnp.float32)


def _dot_nt(a, b):
    return lax.dot_general(_mx(a), _mx(b), (((1,), (1,)), ((), ())),
                           preferred_element_type=jnp.float32)


def _dot_tn(a, b):
    return lax.dot_general(_mx(a), _mx(b), (((0,), (0,)), ((), ())),
                           preferred_element_type=jnp.float32)


def _rms(x, g):
    ms = jnp.mean(x * x, axis=-1, keepdims=True)
    return x * lax.rsqrt(ms + NORM_EPS) * g


def _params(*sem):
    return pltpu.CompilerParams(dimension_semantics=sem, vmem_limit_bytes=VMEM_LIMIT)


def _ffn_kernel(*refs, mixer_out, final_norm, chunks):
    if mixer_out:
        x_ref, y_ref, wo_ref, g_ref, wgu_ref, wd_ref, fg_ref, o_ref, act_ref = refs
        x = x_ref[...] + jnp.dot(y_ref[...], wo_ref[...], preferred_element_type=jnp.float32)
    else:
        x_ref, g_ref, wgu_ref, wd_ref, fg_ref, o_ref, act_ref = refs
        x = x_ref[...]
    f = wd_ref.shape[0]
    h = _mx(_rms(x, g_ref[...]))
    for lo, hi in chunks:
        gate = jnp.dot(h, wgu_ref[:, lo:hi], preferred_element_type=jnp.float32)
        up = jnp.dot(h, wgu_ref[:, f + lo:f + hi], preferred_element_type=jnp.float32)
        act_ref[:, lo:hi] = _mx(gate * jax.nn.sigmoid(gate) * up)
    y = x + 0.5 * jnp.dot(act_ref[...], wd_ref[...], preferred_element_type=jnp.float32)
    if final_norm:
        y = _rms(y, fg_ref[...])
    o_ref[...] = y


def _resident(shape):
    return pl.BlockSpec(shape, lambda i: (0,) * len(shape), pipeline_mode=pl.Buffered(1))


def _ffn(x2, g, w_gu, w_d, final_g, y2=None, w_o=None, *, final_norm, tm):
    t, d = x2.shape
    assert t % tm == 0
    f = w_d.shape[0]
    chunks = tuple((lo, min(lo + FFN_CHUNK, f)) for lo in range(0, f, FFN_CHUNK))
    tok = pl.BlockSpec((tm, d), lambda i: (i, 0))
    mixer_out = y2 is not None
    args, in_specs = [x2], [tok]
    if mixer_out:
        args += [y2, w_o]
        in_specs += [tok, _resident((d, d))]
    args += [g.reshape(1, d), w_gu, w_d, final_g.reshape(1, d)]
    in_specs += [_resident((1, d)), _resident((d, 2 * f)), _resident((f, d)), _resident((1, d))]
    return pl.pallas_call(
        functools.partial(_ffn_kernel, mixer_out=mixer_out, final_norm=final_norm, chunks=chunks),
        grid=(t // tm,),
        in_specs=in_specs,
        out_specs=tok,
        out_shape=jax.ShapeDtypeStruct((t, d), jnp.float32),
        scratch_shapes=[pltpu.VMEM((tm, f), _MXU_DTYPE)],
        compiler_params=_params("parallel"),
        name="ffn",
    )(*args)


def _conv_kernel(x_ref, g_ref, win_ref, cw_ref, o_ref, zbuf_ref):
    tm, d = x_ref.shape[1], x_ref.shape[2]

    @pl.when(pl.program_id(1) == 0)
    def _():
        zbuf_ref[0:SUBLANES, :] = jnp.zeros((SUBLANES, d), jnp.float32)

    h = _mx(_rms(x_ref[0], g_ref[...]))
    gb = jnp.dot(h, win_ref[:, 0:d], preferred_element_type=jnp.float32)
    gc = jnp.dot(h, win_ref[:, d:2 * d], preferred_element_type=jnp.float32)
    hh = jnp.dot(h, win_ref[:, 2 * d:3 * d], preferred_element_type=jnp.float32)
    z = gc * hh
    zbuf_ref[SUBLANES:SUBLANES + tm, :] = z
    z1 = zbuf_ref[SUBLANES - 1:SUBLANES - 1 + tm, :]
    z2 = zbuf_ref[SUBLANES - 2:SUBLANES - 2 + tm, :]
    conv = cw_ref[2:3, :] * z + cw_ref[1:2, :] * z1 + cw_ref[0:1, :] * z2
    zbuf_ref[0:SUBLANES, :] = z[tm - SUBLANES:tm, :]
    o_ref[0] = _mx(gb * conv)


def _conv_mixer(x, g, w_in, conv_w, *, tm):
    b, s, d = x.shape
    assert s % tm == 0
    return pl.pallas_call(
        _conv_kernel,
        grid=(b, s // tm),
        in_specs=[
            pl.BlockSpec((1, tm, d), lambda i, j: (i, j, 0)),
            pl.BlockSpec((1, d), lambda i, j: (0, 0)),
            pl.BlockSpec((d, 3 * d), lambda i, j: (0, 0)),
            pl.BlockSpec(conv_w.shape, lambda i, j: (0, 0)),
        ],
        out_specs=pl.BlockSpec((1, tm, d), lambda i, j: (i, j, 0)),
        out_shape=jax.ShapeDtypeStruct((b, s, d), _MXU_DTYPE),
        scratch_shapes=[pltpu.VMEM((tm + SUBLANES, d), jnp.float32)],
        compiler_params=_params("arbitrary", "arbitrary"),
        name="conv_mixer",
    )(x, g.reshape(1, d), w_in, conv_w)


def _head_sums(x, bd):
    d = x.shape[1]
    parts = [_dot(x[:, c:c + WKV_GROUP], bd) for c in range(0, d, WKV_GROUP)]
    return jnp.concatenate(parts, axis=1)


def _split3(x):
    hi = _mx(x)
    r1 = x - hi.astype(jnp.float32)
    mid = _mx(r1)
    lo = _mx(r1 - mid.astype(jnp.float32))
    return hi, mid, lo


def _proj_kernel(*refs, value_residual):
    if value_residual:
        (x_ref, g_ref, mu_ref, wr_ref, wk_ref, wv_ref, w0_ref, w1_ref, w2_ref,
         a0_ref, a1_ref, a2_ref, g1_ref, g2_ref, kk_ref, ka_ref, tri_ref, bd_ref,
         v0_ref, v1_ref, v2_ref, vf_ref,
         rt_ref, at_ref, bt_ref, kt_ref, v_ref, gate_ref, pl_ref, hbuf_ref) = refs
    else:
        (x_ref, g_ref, mu_ref, wr_ref, wk_ref, wv_ref, w0_ref, w1_ref, w2_ref,
         a0_ref, a1_ref, a2_ref, g1_ref, g2_ref, kk_ref, ka_ref, tri_ref, bd_ref,
         rt_ref, at_ref, bt_ref, kt_ref, v_ref, gate_ref, pl_ref, vf_ref, hbuf_ref) = refs
    tm, d = x_ref.shape[1], x_ref.shape[2]

    @pl.when(pl.program_id(1) == 0)
    def _():
        hbuf_ref[0:SUBLANES, :] = jnp.zeros((SUBLANES, d), jnp.float32)

    h = _rms(x_ref[0], g_ref[...])
    hbuf_ref[SUBLANES:SUBLANES + tm, :] = h
    xx = hbuf_ref[SUBLANES - 1:SUBLANES - 1 + tm, :] - h
    hbuf_ref[0:SUBLANES, :] = h[tm - SUBLANES:tm, :]

    def mix(p):
        return _mx(h + xx * mu_ref[p:p + 1, :])

    r = jnp.dot(mix(0), wr_ref[...], preferred_element_type=jnp.float32)
    k = jnp.dot(mix(1), wk_ref[...], preferred_element_type=jnp.float32)
    xv = mix(2)
    v = jnp.dot(xv, wv_ref[...], preferred_element_type=jnp.float32)
    zw = w0_ref[...] + _dot(jnp.tanh(jnp.dot(mix(3), w1_ref[...],
                                             preferred_element_type=jnp.float32)), w2_ref[...])
    lw = -DECAY_SCALE * jax.nn.sigmoid(zw)
    asig = jax.nn.sigmoid(a0_ref[...] + _dot(jnp.dot(mix(4), a1_ref[...],
                                                     preferred_element_type=jnp.float32),
                                             a2_ref[...]))
    gate_ref[0] = _mx(_dot(jax.nn.sigmoid(jnp.dot(mix(5), g1_ref[...],
                                                  preferred_element_type=jnp.float32)),
                           g2_ref[...]))

    kk = k * kk_ref[...]
    kk = kk * lax.rsqrt(jnp.maximum(_head_sums(kk * kk, bd_ref[...]), 1e-24))
    kp = k * (1.0 + (asig - 1.0) * ka_ref[...])
    if value_residual:
        mixv = jax.nn.sigmoid(v0_ref[...] + _dot(jnp.dot(xv, v1_ref[...],
                                                         preferred_element_type=jnp.float32),
                                                 v2_ref[...]))
        v = v + (vf_ref[0] - v) * mixv
    else:
        vf_ref[0] = v
    v_ref[0] = _mx(v)

    tri = tri_ref[...]
    hi, mid, lo = _split3(lw)
    c = (jnp.dot(tri, hi, preferred_element_type=jnp.float32)
         + jnp.dot(tri, mid, preferred_element_type=jnp.float32)
         + jnp.dot(tri, lo, preferred_element_type=jnp.float32))
    pinc = jnp.exp(c)
    pinv = jnp.exp(-c)
    rt_ref[0] = _mx(r * pinc)
    at_ref[0] = _mx(-kk * jnp.exp(c - lw))
    bt_ref[0] = _mx(kk * asig * pinv)
    kt_ref[0] = _mx(kp * pinv)
    for ci in range(tm // WKV_CHUNK):
        row = ci * WKV_CHUNK + WKV_CHUNK - 1
        pl_ref[0, ci] = pinc[row:row + 1, :]


def _pad_cols(w, n):
    return jnp.pad(w, ((0, 0), (0, n - w.shape[1])))


def _pad_rows(w, n):
    return jnp.pad(w, ((0, n - w.shape[0]), (0, 0)))


def _lora_width(n):
    return -(-n // LANES) * LANES


def _rwkv_proj(x, g, mu, w_rkv, w0, w1, w2, a0, a1, a2, g1, g2, k_k, k_a, v_res, v_first, *, tm):
    b, s, d = x.shape
    assert s % tm == 0 and tm % WKV_CHUNK == 0
    row = lambda a: a.reshape(1, d)
    cst = lambda shape: pl.BlockSpec(shape, lambda i, j: (0,) * len(shape))
    tok = pl.BlockSpec((1, tm, d), lambda i, j: (i, j, 0))

    def lora(wa, wb):
        n = _lora_width(wa.shape[1])
        return _mx(_pad_cols(wa, n)), _mx(_pad_rows(wb, n))

    w1p, w2p = lora(w1, w2)
    a1p, a2p = lora(a1, a2)
    g1p, g2p = lora(g1, g2)
    idx = jnp.arange(tm)
    tri = ((idx[:, None] >= idx[None, :])
           & (idx[:, None] // WKV_CHUNK == idx[None, :] // WKV_CHUNK)).astype(_MXU_DTYPE)
    bd = _head_block_diag(WKV_GROUP)

    args = [x, row(g), mu, _mx(w_rkv[0]), _mx(w_rkv[1]), _mx(w_rkv[2]), row(w0), w1p, w2p,
            row(a0), a1p, a2p, g1p, g2p, row(k_k), row(k_a), tri, bd]
    in_specs = [tok] + [cst(a.shape) for a in args[1:]]
    if v_res is not None:
        v0, v1, v2 = v_res
        v1p, v2p = lora(v1, v2)
        extra = [row(v0), v1p, v2p]
        args += extra + [v_first]
        in_specs += [cst(a.shape) for a in extra] + [tok]

    nchunk = tm // WKV_CHUNK
    out_specs = [tok] * 6 + [pl.BlockSpec((1, nchunk, 1, d), lambda i, j: (i, j, 0, 0))]
    out_shape = ([jax.ShapeDtypeStruct((b, s, d), _MXU_DTYPE)] * 6
                 + [jax.ShapeDtypeStruct((b, s // WKV_CHUNK, 1, d), jnp.float32)])
    if v_res is None:
        out_specs.append(tok)
        out_shape.append(jax.ShapeDtypeStruct((b, s, d), jnp.float32))
    outs = pl.pallas_call(
        functools.partial(_proj_kernel, value_residual=v_res is not None),
        grid=(b, s // tm),
        in_specs=in_specs,
        out_specs=out_specs,
        out_shape=out_shape,
        scratch_shapes=[pltpu.VMEM((tm + SUBLANES, d), jnp.float32)],
        compiler_params=_params("arbitrary", "arbitrary"),
        name="rwkv_proj",
    )(*args)
    return outs


def _stack(x, bd):
    xb = _mx(x)
    return jnp.concatenate([xb] * (bd.shape[0] // xb.shape[0]), axis=0) * bd


def _wkv_prep(rt_ref, at_ref, bt_ref, kt_ref, v_ref, bd, msk_ref, results):
    L, W = WKV_CHUNK, WKV_PREP_GROUP
    m_strict = msk_ref[0]
    m_incl = msk_ref[1]
    eye = msk_ref[2]
    nlev = msk_ref.shape[0] - 4
    nb, rows, lanes = rt_ref.shape
    st = lambda x: _stack(x, bd)
    for b in range(nb):
        where = [(b, slice(c * L, (c + 1) * L), slice(g * W, (g + 1) * W))
                 for c in range(rows // L) for g in range(lanes // W)]
        n = range(len(where))
        rt = [rt_ref[w] for w in where]
        at = [at_ref[w] for w in where]
        vs = [st(v_ref[w]) for w in where]
        bk = [jnp.concatenate([st(bt_ref[w]), st(kt_ref[w])], axis=0) for w in where]
        yield
        aa = [_dot_nt(jnp.concatenate([at[i], rt[i]], axis=0), bk[i]) for i in n]
        a_ab = [aa[i][:L, :W] for i in n]
        a_ak = [aa[i][:L, W:] * m_strict for i in n]
        a_rk = [aa[i][L:, W:] * m_incl for i in n]
        a_rb = [_mx(aa[i][L:, :W] * m_incl) for i in n]
        yield
        T = [eye + a_ab[i] * msk_ref[3] for i in n]
        for lev in range(nlev):
            w1 = [_dot(a_ab[i] * msk_ref[4 + lev], st(T[i])) for i in n]
            yield
            T = [T[i] + _dot(T[i], st(w1[i])) for i in n]
            yield
        avy = [_dot(jnp.concatenate([a_ak[i], a_rk[i]], axis=0), vs[i]) for i in n]
        yield
        wu = [_dot(T[i], jnp.concatenate([st(at[i]), st(avy[i][:L])], axis=1)) for i in n]
        yield
        for i, w in enumerate(where):
            results.append((w, _mx(wu[i][:, :W]), _mx(wu[i][:, W:]), _mx(avy[i][L:]), a_rb[i]))
        yield


def _wkv_scan(wm_ref, uv_ref, y0_ref, arb_ref, rt_ref, bt_ref, kt_ref, v_ref, gate_ref,
              pl_ref, rk_ref, lnw_ref, lnb_ref, bd, o_ref, g_ref):
    L, W = WKV_CHUNK, WKV_GROUP
    nchunk = rt_ref.shape[1] // L
    chains = [(i, slice(g * W, (g + 1) * W))
              for i in range(rt_ref.shape[0]) for g in range(rt_ref.shape[2] // W)]
    nc = range(len(chains))
    st = lambda x: _stack(x, bd)
    G = [g_ref[n] for n in nc]
    same_head = bd.astype(jnp.float32)
    ys = [[] for _ in chains]
    for c in range(nchunk):
        at_ = [(i, slice(c * L, (c + 1) * L), lanes) for i, lanes in chains]
        ur = [_dot_nt(jnp.concatenate([wm_ref[at_[n]], rt_ref[at_[n]]], axis=0), G[n]) for n in nc]
        bk = [jnp.concatenate([bt_ref[at_[n]], kt_ref[at_[n]]], axis=0) for n in nc]
        yield
        u = [_mx(ur[n][:L] + uv_ref[at_[n]]) for n in nc]
        upd = [_dot_tn(jnp.concatenate([u[n], v_ref[at_[n]]], axis=0), bk[n]) for n in nc]
        for n in nc:
            ys[n].append(ur[n][L:] + _dot(arb_ref[at_[n]], st(u[n])) + y0_ref[at_[n]])
        yield
        G = [(G[n] + upd[n] * same_head) * pl_ref[chains[n][0], c][:, chains[n][1]] for n in nc]
    for n in nc:
        g_ref[n] = G[n]

    for n, (i, lanes) in enumerate(chains):
        y = jnp.concatenate(ys[n], axis=0)
        mean = _dot(y, bd) * (1.0 / HEAD_SIZE)
        yc = y - mean
        var = _dot(yc * yc, bd) * (1.0 / HEAD_SIZE)
        yn = yc * lax.rsqrt(var + GN_EPS) * lnw_ref[:, lanes] + lnb_ref[:, lanes]
        rkk = rt_ref[i, :, lanes].astype(jnp.float32) * kt_ref[i, :, lanes] * rk_ref[:, lanes]
        bonus = _dot(rkk, bd) * v_ref[i, :, lanes]
        o_ref[i, :, lanes] = ((yn + bonus) * gate_ref[i, :, lanes]).astype(o_ref.dtype)
        yield


def _wkv_kernel(rtp_ref, atp_ref, btp_ref, ktp_ref, vp_ref,
                rt_ref, bt_ref, kt_ref, v_ref, gate_ref, pl_ref,
                rk_ref, lnw_ref, lnb_ref, bdp_ref, msk_ref, bd_ref,
                o_ref, wm_ref, uv_ref, y0_ref, arb_ref, g_ref):
    k = pl.program_id(1)

    @pl.when(k == 0)
    def _():
        for ref in (wm_ref, uv_ref, y0_ref, arb_ref):
            ref[...] = jnp.zeros_like(ref)

    @pl.when(k <= 1)
    def _():
        g_ref[...] = jnp.zeros_like(g_ref)

    results = []
    scan = _wkv_scan(wm_ref, uv_ref, y0_ref, arb_ref, rt_ref, bt_ref, kt_ref, v_ref, gate_ref,
                     pl_ref, rk_ref, lnw_ref, lnb_ref, bd_ref[...], o_ref, g_ref)
    prep = _wkv_prep(rtp_ref, atp_ref, btp_ref, ktp_ref, vp_ref, bdp_ref[...], msk_ref, results)
    scan_live = prep_live = True
    while scan_live or prep_live:
        if scan_live:
            scan_live = next(scan, _DONE) is not _DONE
        for _ in range(WKV_PREP_STAGES_PER_SCAN_STAGE):
            if prep_live:
                prep_live = next(prep, _DONE) is not _DONE
    for w, wm, uv, y0, arb in results:
        wm_ref[w], uv_ref[w], y0_ref[w], arb_ref[w] = wm, uv, y0, arb


def _wkv_masks(lanes):
    L = WKV_CHUNK
    t = jnp.arange(L)[:, None]
    s = (jnp.arange(lanes) % L)[None, :]
    masks = [s < t, s <= t, s == t]
    n = 1
    while n < L:
        masks.append((s < t) & (t // (2 * n) == s // (2 * n)) & (t // n != s // n))
        n *= 2
    return jnp.stack(masks).astype(jnp.float32)


def _head_block_diag(lanes):
    head = jnp.arange(lanes) // HEAD_SIZE
    return (head[:, None] == head[None, :]).astype(_MXU_DTYPE)


def _wkv(rt, at, bt, kt, v, gate, pl_arr, r_k, ln_w, ln_b):
    b, s, d = rt.shape
    assert WKV_CHUNK == HEAD_SIZE
    bd_prep = _head_block_diag(WKV_PREP_GROUP)
    masks = _wkv_masks(WKV_PREP_GROUP)
    bd = _head_block_diag(WKV_GROUP)
    blk = min(WKV_BLOCK, s)
    assert s % blk == 0 and blk % WKV_CHUNK == 0
    nblk = s // blk
    nb = WKV_BATCH if b % WKV_BATCH == 0 else 1
    cur = pl.BlockSpec((nb, blk, d), lambda i, k: (i, jnp.minimum(k, nblk - 1), 0))
    prev = pl.BlockSpec((nb, blk, d), lambda i, k: (i, jnp.maximum(k - 1, 0), 0))
    cst = lambda a: pl.BlockSpec(a.shape, lambda i, k: (0,) * a.ndim)
    rows = [a.reshape(1, d) for a in (r_k, ln_w, ln_b)]
    consts = rows + [bd_prep, masks, bd]
    return pl.pallas_call(
        _wkv_kernel,
        grid=(b // nb, nblk + 1),
        in_specs=[cur] * 5 + [prev] * 5 + [
            pl.BlockSpec((nb, blk // WKV_CHUNK, 1, d), lambda i, k: (i, jnp.maximum(k - 1, 0), 0, 0)),
        ] + [cst(a) for a in consts],
        out_specs=prev,
        out_shape=jax.ShapeDtypeStruct((b, s, d), _MXU_DTYPE),
        scratch_shapes=[pltpu.VMEM((nb, blk, d), _MXU_DTYPE)] * 4
        + [pltpu.VMEM((nb * (d // WKV_GROUP), WKV_GROUP, WKV_GROUP), jnp.float32)],
        compiler_params=_params("arbitrary", "arbitrary"),
        name="wkv7",
    )(rt, at, bt, kt, v, rt, bt, kt, v, gate, pl_arr, *consts)


def _tile(n, want):
    return min(n, want)


def kernel(x, norm_g, final_g, ffn_w_gu, ffn_w_d, conv_w_in, conv_w, conv_w_out, rwkv_mu, rwkv_w_rkv, rwkv_w0, rwkv_w1, rwkv_w2, rwkv_a0, rwkv_a1, rwkv_a2, rwkv_g1, rwkv_g2, rwkv_k_k, rwkv_k_a, rwkv_r_k, rwkv_ln_w, rwkv_ln_b, rwkv_w_o, rwkv_v0, rwkv_v1, rwkv_v2):
    b, s, d = x.shape
    depth = norm_g.shape[0]
    f = ffn_w_d.shape[2]
    assert d % WKV_GROUP == 0 and WKV_GROUP % WKV_PREP_GROUP == 0
    assert s % WKV_CHUNK == 0
    t = b * s
    ffn_tm = _tile(t, 1024)
    w_gu = _mx(ffn_w_gu)
    w_d = _mx(ffn_w_d)

    def ffn(x3, i, half, mix=None, final_norm=False):
        y2, w_o = (None, None) if mix is None else (mix[0].reshape(t, d), mix[1])
        y = _ffn(x3.reshape(t, d), norm_g[i, 2 * half], w_gu[i, half], w_d[i, half], final_g,
                 y2, w_o, final_norm=final_norm, tm=ffn_tm)
        return y.reshape(b, s, d)

    v_first = None
    for i in range(depth):
        x = ffn(x, i, 0)
        if i % 2 == 0:
            c = i // 2
            y = _conv_mixer(x, norm_g[i, 1], _mx(conv_w_in[c]), conv_w[c], tm=_tile(s, 1024))
            w_o = _mx(conv_w_out[c])
        else:
            j = i // 2
            v_res = None if j == 0 else (rwkv_v0[j - 1], rwkv_v1[j - 1], rwkv_v2[j - 1])
            outs = _rwkv_proj(
                x, norm_g[i, 1], rwkv_mu[j], rwkv_w_rkv[j], rwkv_w0[j], rwkv_w1[j], rwkv_w2[j],
                rwkv_a0[j], rwkv_a1[j], rwkv_a2[j], rwkv_g1[j], rwkv_g2[j],
                rwkv_k_k[j], rwkv_k_a[j], v_res, v_first, tm=_tile(s, 256))
            rt, at, bt, kt, v, gate, pl_arr = outs[:7]
            if j == 0:
                v_first = outs[7]
            y = _wkv(rt, at, bt, kt, v, gate, pl_arr, rwkv_r_k[j].reshape(-1),
                     rwkv_ln_w[j], rwkv_ln_b[j])
            w_o = _mx(rwkv_w_o[j])
        x = ffn(x, i, 1, mix=(y, w_o), final_norm=(i == depth - 1))
    return x
```

```python
import functools
import math

import jax
import jax.numpy as jnp
from jax import lax
from jax.experimental import pallas as pl
from jax.experimental.pallas import tpu as pltpu

NORM_EPS = 1e-5
GN_EPS = 64e-5
HEAD_SIZE = 64
WKV_CHUNK = 64
WKV_GROUP = 256
WKV_PREP_GROUP = 128
WKV_BLOCK = 256
WKV_BATCH = 2
WKV_PREP_STAGES_PER_SCAN_STAGE = 2
FFN_CHUNK = 512
SUBLANES = 8
LANES = 128
DECAY_SCALE = math.exp(-0.5)
VMEM_LIMIT = 56 * 1024 * 1024

_MXU_DTYPE = jnp.bfloat16
_DONE = object()


def _mx(x):
    return x.astype(_MXU_DTYPE)


def _dot(a, b):
    return jnp.dot(_mx(a), _mx(b), preferred_element_type=jnp.float32)


def _dot_nt(a, b):
    return lax.dot_general(_mx(a), _mx(b), (((1,), (1,)), ((), ())),
                           preferred_element_type=jnp.float32)


def _dot_tn(a, b):
    return lax.dot_general(_mx(a), _mx(b), (((0,), (0,)), ((), ())),
                           preferred_element_type=jnp.float32)


def _rms(x, g):
    ms = jnp.mean(x * x, axis=-1, keepdims=True)
    return x * lax.rsqrt(ms + NORM_EPS) * g


def _params(*sem):
    return pltpu.CompilerParams(dimension_semantics=sem, vmem_limit_bytes=VMEM_LIMIT)


def _ffn_kernel(*refs, mixer_out, final_norm, chunks):
    if mixer_out:
        x_ref, y_ref, wo_ref, g_ref, wgu_ref, wd_ref, fg_ref, o_ref, act_ref = refs
        x = x_ref[...] + jnp.dot(y_ref[...], wo_ref[...], preferred_element_type=jnp.float32)
    else:
        x_ref, g_ref, wgu_ref, wd_ref, fg_ref, o_ref, act_ref = refs
        x = x_ref[...]
    f = wd_ref.shape[0]
    h = _mx(_rms(x, g_ref[...]))
    for lo, hi in chunks:
        gate = jnp.dot(h, wgu_ref[:, lo:hi], preferred_element_type=jnp.float32)
        up = jnp.dot(h, wgu_ref[:, f + lo:f + hi], preferred_element_type=jnp.float32)
        act_ref[:, lo:hi] = _mx(gate * jax.nn.sigmoid(gate) * up)
    y = x + 0.5 * jnp.dot(act_ref[...], wd_ref[...], preferred_element_type=jnp.float32)
    if final_norm:
        y = _rms(y, fg_ref[...])
    o_ref[...] = y


def _resident(shape):
    return pl.BlockSpec(shape, lambda i: (0,) * len(shape), pipeline_mode=pl.Buffered(1))


def _ffn(x2, g, w_gu, w_d, which, final_g, y2=None, w_o=None, *, final_norm, tm):
    t, d = x2.shape
    assert t % tm == 0
    f = w_d.shape[2]
    chunks = tuple((lo, min(lo + FFN_CHUNK, f)) for lo in range(0, f, FFN_CHUNK))
    tok = pl.BlockSpec((tm, d), lambda i: (i, 0))
    layer_weights = lambda rows, cols: pl.BlockSpec((None, None, rows, cols),
                                                    lambda i: (*which, 0, 0),
                                                    pipeline_mode=pl.Buffered(1))
    mixer_out = y2 is not None
    args, in_specs = [x2], [tok]
    if mixer_out:
        args += [y2, w_o]
        in_specs += [tok, _resident((d, d))]
    args += [g.reshape(1, d), w_gu, w_d, final_g.reshape(1, d)]
    in_specs += [_resident((1, d)), layer_weights(d, 2 * f), layer_weights(f, d), _resident((1, d))]
    return pl.pallas_call(
        functools.partial(_ffn_kernel, mixer_out=mixer_out, final_norm=final_norm, chunks=chunks),
        grid=(t // tm,),
        in_specs=in_specs,
        out_specs=tok,
        out_shape=jax.ShapeDtypeStruct((t, d), jnp.float32),
        scratch_shapes=[pltpu.VMEM((tm, f), _MXU_DTYPE)],
        compiler_params=_params("parallel"),
        name="ffn",
    )(*args)


def _conv_kernel(x_ref, g_ref, win_ref, cw_ref, o_ref, zbuf_ref):
    tm, d = x_ref.shape[1], x_ref.shape[2]

    @pl.when(pl.program_id(1) == 0)
    def _():
        zbuf_ref[0:SUBLANES, :] = jnp.zeros((SUBLANES, d), jnp.float32)

    h = _mx(_rms(x_ref[0], g_ref[...]))
    gb = jnp.dot(h, win_ref[:, 0:d], preferred_element_type=jnp.float32)
    gc = jnp.dot(h, win_ref[:, d:2 * d], preferred_element_type=jnp.float32)
    hh = jnp.dot(h, win_ref[:, 2 * d:3 * d], preferred_element_type=jnp.float32)
    z = gc * hh
    zbuf_ref[SUBLANES:SUBLANES + tm, :] = z
    z1 = zbuf_ref[SUBLANES - 1:SUBLANES - 1 + tm, :]
    z2 = zbuf_ref[SUBLANES - 2:SUBLANES - 2 + tm, :]
    conv = cw_ref[2:3, :] * z + cw_ref[1:2, :] * z1 + cw_ref[0:1, :] * z2
    zbuf_ref[0:SUBLANES, :] = z[tm - SUBLANES:tm, :]
    o_ref[0] = _mx(gb * conv)


def _conv_mixer(x, g, w_in, conv_w, *, tm):
    b, s, d = x.shape
    assert s % tm == 0
    return pl.pallas_call(
        _conv_kernel,
        grid=(b, s // tm),
        in_specs=[
            pl.BlockSpec((1, tm, d), lambda i, j: (i, j, 0)),
            pl.BlockSpec((1, d), lambda i, j: (0, 0)),
            pl.BlockSpec((d, 3 * d), lambda i, j: (0, 0)),
            pl.BlockSpec(conv_w.shape, lambda i, j: (0, 0)),
        ],
        out_specs=pl.BlockSpec((1, tm, d), lambda i, j: (i, j, 0)),
        out_shape=jax.ShapeDtypeStruct((b, s, d), _MXU_DTYPE),
        scratch_shapes=[pltpu.VMEM((tm + SUBLANES, d), jnp.float32)],
        compiler_params=_params("arbitrary", "arbitrary"),
        name="conv_mixer",
    )(x, g.reshape(1, d), w_in, conv_w)


def _head_sums(x, bd):
    d = x.shape[1]
    parts = [_dot(x[:, c:c + WKV_GROUP], bd) for c in range(0, d, WKV_GROUP)]
    return jnp.concatenate(parts, axis=1)


def _split3(x):
    hi = _mx(x)
    r1 = x - hi.astype(jnp.float32)
    mid = _mx(r1)
    lo = _mx(r1 - mid.astype(jnp.float32))
    return hi, mid, lo


def _proj_kernel(*refs, value_residual):
    if value_residual:
        (x_ref, g_ref, mu_ref, wr_ref, wk_ref, wv_ref, w0_ref, w1_ref, w2_ref,
         a0_ref, a1_ref, a2_ref, g1_ref, g2_ref, kk_ref, ka_ref, tri_ref, bd_ref,
         v0_ref, v1_ref, v2_ref, vf_ref,
         rt_ref, at_ref, bt_ref, kt_ref, v_ref, gate_ref, pl_ref, hbuf_ref) = refs
    else:
        (x_ref, g_ref, mu_ref, wr_ref, wk_ref, wv_ref, w0_ref, w1_ref, w2_ref,
         a0_ref, a1_ref, a2_ref, g1_ref, g2_ref, kk_ref, ka_ref, tri_ref, bd_ref,
         rt_ref, at_ref, bt_ref, kt_ref, v_ref, gate_ref, pl_ref, vf_ref, hbuf_ref) = refs
    tm, d = x_ref.shape[1], x_ref.shape[2]

    @pl.when(pl.program_id(1) == 0)
    def _():
        hbuf_ref[0:SUBLANES, :] = jnp.zeros((SUBLANES, d), jnp.float32)

    h = _rms(x_ref[0], g_ref[...])
    hbuf_ref[SUBLANES:SUBLANES + tm, :] = h
    xx = hbuf_ref[SUBLANES - 1:SUBLANES - 1 + tm, :] - h
    hbuf_ref[0:SUBLANES, :] = h[tm - SUBLANES:tm, :]

    def mix(p):
        return _mx(h + xx * mu_ref[p:p + 1, :])

    r = jnp.dot(mix(0), wr_ref[...], preferred_element_type=jnp.float32)
    k = jnp.dot(mix(1), wk_ref[...], preferred_element_type=jnp.float32)
    xv = mix(2)
    v = jnp.dot(xv, wv_ref[...], preferred_element_type=jnp.float32)
    zw = w0_ref[...] + _dot(jnp.tanh(jnp.dot(mix(3), w1_ref[...],
                                             preferred_element_type=jnp.float32)), w2_ref[...])
    lw = -DECAY_SCALE * jax.nn.sigmoid(zw)
    asig = jax.nn.sigmoid(a0_ref[...] + _dot(jnp.dot(mix(4), a1_ref[...],
                                                     preferred_element_type=jnp.float32),
                                             a2_ref[...]))
    gate_ref[0] = _mx(_dot(jax.nn.sigmoid(jnp.dot(mix(5), g1_ref[...],
                                                  preferred_element_type=jnp.float32)),
                           g2_ref[...]))

    kk = k * kk_ref[...]
    kk = kk * lax.rsqrt(jnp.maximum(_head_sums(kk * kk, bd_ref[...]), 1e-24))
    kp = k * (1.0 + (asig - 1.0) * ka_ref[...])
    if value_residual:
        mixv = jax.nn.sigmoid(v0_ref[...] + _dot(jnp.dot(xv, v1_ref[...],
                                                         preferred_element_type=jnp.float32),
                                                 v2_ref[...]))
        v = v + (vf_ref[0] - v) * mixv
    else:
        vf_ref[0] = v
    v_ref[0] = _mx(v)

    tri = tri_ref[...]
    hi, mid, lo = _split3(lw)
    c = (jnp.dot(tri, hi, preferred_element_type=jnp.float32)
         + jnp.dot(tri, mid, preferred_element_type=jnp.float32)
         + jnp.dot(tri, lo, preferred_element_type=jnp.float32))
    pinc = jnp.exp(c)
    pinv = jnp.exp(-c)
    rt_ref[0] = _mx(r * pinc)
    at_ref[0] = _mx(-kk * jnp.exp(c - lw))
    bt_ref[0] = _mx(kk * asig * pinv)
    kt_ref[0] = _mx(kp * pinv)
    for ci in range(tm // WKV_CHUNK):
        row = ci * WKV_CHUNK + WKV_CHUNK - 1
        pl_ref[0, ci] = pinc[row:row + 1, :]


def _pad_cols(w, n):
    return jnp.pad(w, ((0, 0), (0, n - w.shape[1])))


def _pad_rows(w, n):
    return jnp.pad(w, ((0, n - w.shape[0]), (0, 0)))


def _lora_width(n):
    return -(-n // LANES) * LANES


def _rwkv_proj(x, g, mu, w_rkv, w0, w1, w2, a0, a1, a2, g1, g2, k_k, k_a, v_res, v_first, *, tm):
    b, s, d = x.shape
    assert s % tm == 0 and tm % WKV_CHUNK == 0
    row = lambda a: a.reshape(1, d)
    cst = lambda shape: pl.BlockSpec(shape, lambda i, j: (0,) * len(shape))
    tok = pl.BlockSpec((1, tm, d), lambda i, j: (i, j, 0))

    def lora(wa, wb):
        n = _lora_width(wa.shape[1])
        return _mx(_pad_cols(wa, n)), _mx(_pad_rows(wb, n))

    w1p, w2p = lora(w1, w2)
    a1p, a2p = lora(a1, a2)
    g1p, g2p = lora(g1, g2)
    idx = jnp.arange(tm)
    tri = ((idx[:, None] >= idx[None, :])
           & (idx[:, None] // WKV_CHUNK == idx[None, :] // WKV_CHUNK)).astype(_MXU_DTYPE)
    bd = _head_block_diag(WKV_GROUP)

    args = [x, row(g), mu, _mx(w_rkv[0]), _mx(w_rkv[1]), _mx(w_rkv[2]), row(w0), w1p, w2p,
            row(a0), a1p, a2p, g1p, g2p, row(k_k), row(k_a), tri, bd]
    in_specs = [tok] + [cst(a.shape) for a in args[1:]]
    if v_res is not None:
        v0, v1, v2 = v_res
        v1p, v2p = lora(v1, v2)
        extra = [row(v0), v1p, v2p]
        args += extra + [v_first]
        in_specs += [cst(a.shape) for a in extra] + [tok]

    nchunk = tm // WKV_CHUNK
    out_specs = [tok] * 6 + [pl.BlockSpec((1, nchunk, 1, d), lambda i, j: (i, j, 0, 0))]
    out_shape = ([jax.ShapeDtypeStruct((b, s, d), _MXU_DTYPE)] * 6
                 + [jax.ShapeDtypeStruct((b, s // WKV_CHUNK, 1, d), jnp.float32)])
    if v_res is None:
        out_specs.append(tok)
        out_shape.append(jax.ShapeDtypeStruct((b, s, d), jnp.float32))
    outs = pl.pallas_call(
        functools.partial(_proj_kernel, value_residual=v_res is not None),
        grid=(b, s // tm),
        in_specs=in_specs,
        out_specs=out_specs,
        out_shape=out_shape,
        scratch_shapes=[pltpu.VMEM((tm + SUBLANES, d), jnp.float32)],
        compiler_params=_params("arbitrary", "arbitrary"),
        name="rwkv_proj",
    )(*args)
    return outs


def _stack(x, bd):
    xb = _mx(x)
    return jnp.concatenate([xb] * (bd.shape[0] // xb.shape[0]), axis=0) * bd


def _wkv_prep(rt_ref, at_ref, bt_ref, kt_ref, v_ref, bd, msk_ref, results):
    L, W = WKV_CHUNK, WKV_PREP_GROUP
    m_strict = msk_ref[0]
    m_incl = msk_ref[1]
    eye = msk_ref[2]
    nlev = msk_ref.shape[0] - 4
    nb, rows, lanes = rt_ref.shape
    st = lambda x: _stack(x, bd)
    for b in range(nb):
        where = [(b, slice(c * L, (c + 1) * L), slice(g * W, (g + 1) * W))
                 for c in range(rows // L) for g in range(lanes // W)]
        n = range(len(where))
        rt = [rt_ref[w] for w in where]
        at = [at_ref[w] for w in where]
        vs = [st(v_ref[w]) for w in where]
        bk = [jnp.concatenate([st(bt_ref[w]), st(kt_ref[w])], axis=0) for w in where]
        yield
        aa = [_dot_nt(jnp.concatenate([at[i], rt[i]], axis=0), bk[i]) for i in n]
        a_ab = [aa[i][:L, :W] for i in n]
        a_ak = [aa[i][:L, W:] * m_strict for i in n]
        a_rk = [aa[i][L:, W:] * m_incl for i in n]
        a_rb = [_mx(aa[i][L:, :W] * m_incl) for i in n]
        yield
        T = [eye + a_ab[i] * msk_ref[3] for i in n]
        for lev in range(nlev):
            w1 = [_dot(a_ab[i] * msk_ref[4 + lev], st(T[i])) for i in n]
            yield
            T = [T[i] + _dot(T[i], st(w1[i])) for i in n]
            yield
        avy = [_dot(jnp.concatenate([a_ak[i], a_rk[i]], axis=0), vs[i]) for i in n]
        yield
        wu = [_dot(T[i], jnp.concatenate([st(at[i]), st(avy[i][:L])], axis=1)) for i in n]
        yield
        for i, w in enumerate(where):
            results.append((w, _mx(wu[i][:, :W]), _mx(wu[i][:, W:]), _mx(avy[i][L:]), a_rb[i]))
        yield


def _wkv_scan(wm_ref, uv_ref, y0_ref, arb_ref, rt_ref, bt_ref, kt_ref, v_ref, gate_ref,
              pl_ref, rk_ref, lnw_ref, lnb_ref, bd, o_ref, g_ref):
    L, W = WKV_CHUNK, WKV_GROUP
    nchunk = rt_ref.shape[1] // L
    chains = [(i, slice(g * W, (g + 1) * W))
              for i in range(rt_ref.shape[0]) for g in range(rt_ref.shape[2] // W)]
    nc = range(len(chains))
    st = lambda x: _stack(x, bd)
    G = [g_ref[n] for n in nc]
    same_head = bd.astype(jnp.float32)
    ys = [[] for _ in chains]
    for c in range(nchunk):
        at_ = [(i, slice(c * L, (c + 1) * L), lanes) for i, lanes in chains]
        ur = [_dot_nt(jnp.concatenate([wm_ref[at_[n]], rt_ref[at_[n]]], axis=0), G[n]) for n in nc]
        bk = [jnp.concatenate([bt_ref[at_[n]], kt_ref[at_[n]]], axis=0) for n in nc]
        yield
        u = [_mx(ur[n][:L] + uv_ref[at_[n]]) for n in nc]
        upd = [_dot_tn(jnp.concatenate([u[n], v_ref[at_[n]]], axis=0), bk[n]) for n in nc]
        for n in nc:
            ys[n].append(ur[n][L:] + _dot(arb_ref[at_[n]], st(u[n])) + y0_ref[at_[n]])
        yield
        G = [(G[n] + upd[n] * same_head) * pl_ref[chains[n][0], c][:, chains[n][1]] for n in nc]
    for n in nc:
        g_ref[n] = G[n]

    for n, (i, lanes) in enumerate(chains):
        y = jnp.concatenate(ys[n], axis=0)
        mean = _dot(y, bd) * (1.0 / HEAD_SIZE)
        yc = y - mean
        var = _dot(yc * yc, bd) * (1.0 / HEAD_SIZE)
        yn = yc * lax.rsqrt(var + GN_EPS) * lnw_ref[:, lanes] + lnb_ref[:, lanes]
        rkk = rt_ref[i, :, lanes].astype(jnp.float32) * kt_ref[i, :, lanes] * rk_ref[:, lanes]
        bonus = _dot(rkk, bd) * v_ref[i, :, lanes]
        o_ref[i, :, lanes] = ((yn + bonus) * gate_ref[i, :, lanes]).astype(o_ref.dtype)
        yield


def _wkv_kernel(rtp_ref, atp_ref, btp_ref, ktp_ref, vp_ref,
                rt_ref, bt_ref, kt_ref, v_ref, gate_ref, pl_ref,
                rk_ref, lnw_ref, lnb_ref, bdp_ref, msk_ref, bd_ref,
                o_ref, wm_ref, uv_ref, y0_ref, arb_ref, g_ref):
    k = pl.program_id(1)

    @pl.when(k == 0)
    def _():
        for ref in (wm_ref, uv_ref, y0_ref, arb_ref):
            ref[...] = jnp.zeros_like(ref)

    @pl.when(k <= 1)
    def _():
        g_ref[...] = jnp.zeros_like(g_ref)

    results = []
    scan = _wkv_scan(wm_ref, uv_ref, y0_ref, arb_ref, rt_ref, bt_ref, kt_ref, v_ref, gate_ref,
                     pl_ref, rk_ref, lnw_ref, lnb_ref, bd_ref[...], o_ref, g_ref)
    prep = _wkv_prep(rtp_ref, atp_ref, btp_ref, ktp_ref, vp_ref, bdp_ref[...], msk_ref, results)
    scan_live = prep_live = True
    while scan_live or prep_live:
        if scan_live:
            scan_live = next(scan, _DONE) is not _DONE
        for _ in range(WKV_PREP_STAGES_PER_SCAN_STAGE):
            if prep_live:
                prep_live = next(prep, _DONE) is not _DONE
    for w, wm, uv, y0, arb in results:
        wm_ref[w], uv_ref[w], y0_ref[w], arb_ref[w] = wm, uv, y0, arb


def _wkv_masks(lanes):
    L = WKV_CHUNK
    t = jnp.arange(L)[:, None]
    s = (jnp.arange(lanes) % L)[None, :]
    masks = [s < t, s <= t, s == t]
    n = 1
    while n < L:
        masks.append((s < t) & (t // (2 * n) == s // (2 * n)) & (t // n != s // n))
        n *= 2
    return jnp.stack(masks).astype(jnp.float32)


def _head_block_diag(lanes):
    head = jnp.arange(lanes) // HEAD_SIZE
    return (head[:, None] == head[None, :]).astype(_MXU_DTYPE)


def _wkv(rt, at, bt, kt, v, gate, pl_arr, r_k, ln_w, ln_b):
    b, s, d = rt.shape
    assert WKV_CHUNK == HEAD_SIZE
    bd_prep = _head_block_diag(WKV_PREP_GROUP)
    masks = _wkv_masks(WKV_PREP_GROUP)
    bd = _head_block_diag(WKV_GROUP)
    blk = min(WKV_BLOCK, s)
    assert s % blk == 0 and blk % WKV_CHUNK == 0
    nblk = s // blk
    nb = WKV_BATCH if b % WKV_BATCH == 0 else 1
    cur = pl.BlockSpec((nb, blk, d), lambda i, k: (i, jnp.minimum(k, nblk - 1), 0))
    prev = pl.BlockSpec((nb, blk, d), lambda i, k: (i, jnp.maximum(k - 1, 0), 0))
    cst = lambda a: pl.BlockSpec(a.shape, lambda i, k: (0,) * a.ndim)
    rows = [a.reshape(1, d) for a in (r_k, ln_w, ln_b)]
    consts = rows + [bd_prep, masks, bd]
    return pl.pallas_call(
        _wkv_kernel,
        grid=(b // nb, nblk + 1),
        in_specs=[cur] * 5 + [prev] * 5 + [
            pl.BlockSpec((nb, blk // WKV_CHUNK, 1, d), lambda i, k: (i, jnp.maximum(k - 1, 0), 0, 0)),
        ] + [cst(a) for a in consts],
        out_specs=prev,
        out_shape=jax.ShapeDtypeStruct((b, s, d), _MXU_DTYPE),
        scratch_shapes=[pltpu.VMEM((nb, blk, d), _MXU_DTYPE)] * 4
        + [pltpu.VMEM((nb * (d // WKV_GROUP), WKV_GROUP, WKV_GROUP), jnp.float32)],
        compiler_params=_params("arbitrary", "arbitrary"),
        name="wkv7",
    )(rt, at, bt, kt, v, rt, bt, kt, v, gate, pl_arr, *consts)


def _tile(n, want):
    return min(n, want)


def kernel(x, norm_g, final_g, ffn_w_gu, ffn_w_d, conv_w_in, conv_w, conv_w_out, rwkv_mu, rwkv_w_rkv, rwkv_w0, rwkv_w1, rwkv_w2, rwkv_a0, rwkv_a1, rwkv_a2, rwkv_g1, rwkv_g2, rwkv_k_k, rwkv_k_a, rwkv_r_k, rwkv_ln_w, rwkv_ln_b, rwkv_w_o, rwkv_v0, rwkv_v1, rwkv_v2):
    b, s, d = x.shape
    depth = norm_g.shape[0]
    f = ffn_w_d.shape[2]
    assert d % WKV_GROUP == 0 and WKV_GROUP % WKV_PREP_GROUP == 0
    assert s % WKV_CHUNK == 0
    t = b * s
    ffn_tm = _tile(t, 1024)
    w_gu = _mx(ffn_w_gu)
    w_d = _mx(ffn_w_d)

    def ffn(x3, i, half, mix=None, final_norm=False):
        y2, w_o = (None, None) if mix is None else (mix[0].reshape(t, d), mix[1])
        y = _ffn(x3.reshape(t, d), norm_g[i, 2 * half], w_gu, w_d, (i, half), final_g,
                 y2, w_o, final_norm=final_norm, tm=ffn_tm)
        return y.reshape(b, s, d)

    v_first = None
    for i in range(depth):
        x = ffn(x, i, 0)
        if i % 2 == 0:
            c = i // 2
            y = _conv_mixer(x, norm_g[i, 1], _mx(conv_w_in[c]), conv_w[c], tm=_tile(s, 1024))
            w_o = _mx(conv_w_out[c])
        else:
            j = i // 2
            v_res = None if j == 0 else (rwkv_v0[j - 1], rwkv_v1[j - 1], rwkv_v2[j - 1])
            outs = _rwkv_proj(
                x, norm_g[i, 1], rwkv_mu[j], rwkv_w_rkv[j], rwkv_w0[j], rwkv_w1[j], rwkv_w2[j],
                rwkv_a0[j], rwkv_a1[j], rwkv_a2[j], rwkv_g1[j], rwkv_g2[j],
                rwkv_k_k[j], rwkv_k_a[j], v_res, v_first, tm=_tile(s, 512))
            rt, at, bt, kt, v, gate, pl_arr = outs[:7]
            if j == 0:
                v_first = outs[7]
            y = _wkv(rt, at, bt, kt, v, gate, pl_arr, rwkv_r_k[j].reshape(-1),
                     rwkv_ln_w[j], rwkv_ln_b[j])
            w_o = _mx(rwkv_w_o[j])
        x = ffn(x, i, 1, mix=(y, w_o), final_norm=(i == depth - 1))
    return x
```

```python
import functools
import math

import jax
import jax.numpy as jnp
from jax import lax
from jax.experimental import pallas as pl
from jax.experimental.pallas import tpu as pltpu

NORM_EPS = 1e-5
GN_EPS = 64e-5
HEAD_SIZE = 64
WKV_CHUNK = 64
WKV_GROUP = 256
WKV_PREP_GROUP = 128
WKV_BLOCK = 256
WKV_BATCH = 2
WKV_PREP_STAGES_PER_SCAN_STAGE = 2
FFN_CHUNK = 512
SUBLANES = 8
LANES = 128
DECAY_SCALE = math.exp(-0.5)
VMEM_LIMIT = 56 * 1024 * 1024

_MXU_DTYPE = jnp.bfloat16
_DONE = object()


def _mx(x):
    return x.astype(_MXU_DTYPE)


def _dot(a, b):
    return jnp.dot(_mx(a), _mx(b), preferred_element_type=jnp.float32)


def _dot_nt(a, b):
    return lax.dot_general(_mx(a), _mx(b), (((1,), (1,)), ((), ())),
                           preferred_element_type=jnp.float32)


def _dot_tn(a, b):
    return lax.dot_general(_mx(a), _mx(b), (((0,), (0,)), ((), ())),
                           preferred_element_type=jnp.float32)


def _rms(x, g):
    ms = jnp.mean(x * x, axis=-1, keepdims=True)
    return x * lax.rsqrt(ms + NORM_EPS) * g


def _params(*sem):
    return pltpu.CompilerParams(dimension_semantics=sem, vmem_limit_bytes=VMEM_LIMIT)


def _ffn_kernel(*refs, mixer_out, final_norm, chunks):
    if mixer_out:
        x_ref, y_ref, wo_ref, g_ref, wgu_ref, wd_ref, fg_ref, o_ref, act_ref = refs
        x = x_ref[...] + jnp.dot(y_ref[...], wo_ref[...], preferred_element_type=jnp.float32)
    else:
        x_ref, g_ref, wgu_ref, wd_ref, fg_ref, o_ref, act_ref = refs
        x = x_ref[...]
    f = wd_ref.shape[0]
    h = _mx(_rms(x, g_ref[...]))
    for lo, hi in chunks:
        gate = jnp.dot(h, wgu_ref[:, lo:hi], preferred_element_type=jnp.float32)
        up = jnp.dot(h, wgu_ref[:, f + lo:f + hi], preferred_element_type=jnp.float32)
        act_ref[:, lo:hi] = _mx(gate * jax.nn.sigmoid(gate) * up)
    y = x + 0.5 * jnp.dot(act_ref[...], wd_ref[...], preferred_element_type=jnp.float32)
    if final_norm:
        y = _rms(y, fg_ref[...])
    o_ref[...] = y


def _resident(shape):
    return pl.BlockSpec(shape, lambda i: (0,) * len(shape), pipeline_mode=pl.Buffered(1))


def _ffn(x2, g, w_gu, w_d, which, final_g, y2=None, w_o=None, *, final_norm, tm):
    t, d = x2.shape
    assert t % tm == 0
    f = w_d.shape[2]
    chunks = tuple((lo, min(lo + FFN_CHUNK, f)) for lo in range(0, f, FFN_CHUNK))
    tok = pl.BlockSpec((tm, d), lambda i: (i, 0))
    layer_weights = lambda rows, cols: pl.BlockSpec((None, None, rows, cols),
                                                    lambda i: (*which, 0, 0),
                                                    pipeline_mode=pl.Buffered(1))
    mixer_out = y2 is not None
    args, in_specs = [x2], [tok]
    if mixer_out:
        args += [y2, w_o]
        in_specs += [tok, _resident((d, d))]
    args += [g.reshape(1, d), w_gu, w_d, final_g.reshape(1, d)]
    in_specs += [_resident((1, d)), layer_weights(d, 2 * f), layer_weights(f, d), _resident((1, d))]
    return pl.pallas_call(
        functools.partial(_ffn_kernel, mixer_out=mixer_out, final_norm=final_norm, chunks=chunks),
        grid=(t // tm,),
        in_specs=in_specs,
        out_specs=tok,
        out_shape=jax.ShapeDtypeStruct((t, d), jnp.float32),
        scratch_shapes=[pltpu.VMEM((tm, f), _MXU_DTYPE)],
        compiler_params=_params("parallel"),
        name="ffn",
    )(*args)


def _conv_kernel(x_ref, g_ref, win_ref, cw_ref, o_ref, zbuf_ref):
    tm, d = x_ref.shape[1], x_ref.shape[2]

    @pl.when(pl.program_id(1) == 0)
    def _():
        zbuf_ref[0:SUBLANES, :] = jnp.zeros((SUBLANES, d), jnp.float32)

    h = _mx(_rms(x_ref[0], g_ref[...]))
    gc = jnp.dot(h, win_ref[:, d:2 * d], preferred_element_type=jnp.float32)
    hh = jnp.dot(h, win_ref[:, 2 * d:3 * d], preferred_element_type=jnp.float32)
    z = gc * hh
    zbuf_ref[SUBLANES:SUBLANES + tm, :] = z
    z1 = zbuf_ref[SUBLANES - 1:SUBLANES - 1 + tm, :]
    z2 = zbuf_ref[SUBLANES - 2:SUBLANES - 2 + tm, :]
    conv = cw_ref[2:3, :] * z + cw_ref[1:2, :] * z1 + cw_ref[0:1, :] * z2
    zbuf_ref[0:SUBLANES, :] = z[tm - SUBLANES:tm, :]
    gb = jnp.dot(h, win_ref[:, 0:d], preferred_element_type=jnp.float32)
    o_ref[0] = _mx(gb * conv)


def _conv_mixer(x, g, w_in, conv_w, *, tm):
    b, s, d = x.shape
    assert s % tm == 0
    return pl.pallas_call(
        _conv_kernel,
        grid=(b, s // tm),
        in_specs=[
            pl.BlockSpec((1, tm, d), lambda i, j: (i, j, 0)),
            pl.BlockSpec((1, d), lambda i, j: (0, 0)),
            pl.BlockSpec((d, 3 * d), lambda i, j: (0, 0)),
            pl.BlockSpec(conv_w.shape, lambda i, j: (0, 0)),
        ],
        out_specs=pl.BlockSpec((1, tm, d), lambda i, j: (i, j, 0)),
        out_shape=jax.ShapeDtypeStruct((b, s, d), _MXU_DTYPE),
        scratch_shapes=[pltpu.VMEM((tm + SUBLANES, d), jnp.float32)],
        compiler_params=_params("arbitrary", "arbitrary"),
        name="conv_mixer",
    )(x, g.reshape(1, d), w_in, conv_w)


def _head_sums(x, bd):
    d = x.shape[1]
    parts = [_dot(x[:, c:c + WKV_GROUP], bd) for c in range(0, d, WKV_GROUP)]
    return jnp.concatenate(parts, axis=1)


def _split3(x):
    hi = _mx(x)
    r1 = x - hi.astype(jnp.float32)
    mid = _mx(r1)
    lo = _mx(r1 - mid.astype(jnp.float32))
    return hi, mid, lo


def _proj_kernel(*refs, value_residual):
    if value_residual:
        (x_ref, g_ref, mu_ref, wr_ref, wk_ref, wv_ref, w0_ref, w1_ref, w2_ref,
         a0_ref, a1_ref, a2_ref, g1_ref, g2_ref, kk_ref, ka_ref, tri_ref, bd_ref,
         v0_ref, v1_ref, v2_ref, vf_ref,
         rt_ref, at_ref, bt_ref, kt_ref, v_ref, gate_ref, pl_ref, hbuf_ref) = refs
    else:
        (x_ref, g_ref, mu_ref, wr_ref, wk_ref, wv_ref, w0_ref, w1_ref, w2_ref,
         a0_ref, a1_ref, a2_ref, g1_ref, g2_ref, kk_ref, ka_ref, tri_ref, bd_ref,
         rt_ref, at_ref, bt_ref, kt_ref, v_ref, gate_ref, pl_ref, vf_ref, hbuf_ref) = refs
    tm, d = x_ref.shape[1], x_ref.shape[2]

    @pl.when(pl.program_id(1) == 0)
    def _():
        hbuf_ref[0:SUBLANES, :] = jnp.zeros((SUBLANES, d), jnp.float32)

    h = _rms(x_ref[0], g_ref[...])
    hbuf_ref[SUBLANES:SUBLANES + tm, :] = h
    xx = hbuf_ref[SUBLANES - 1:SUBLANES - 1 + tm, :] - h
    hbuf_ref[0:SUBLANES, :] = h[tm - SUBLANES:tm, :]

    def mix(p):
        return _mx(h + xx * mu_ref[p:p + 1, :])

    k = jnp.dot(mix(1), wk_ref[...], preferred_element_type=jnp.float32)
    zw = w0_ref[...] + _dot(jnp.tanh(jnp.dot(mix(3), w1_ref[...],
                                             preferred_element_type=jnp.float32)), w2_ref[...])
    lw = -DECAY_SCALE * jax.nn.sigmoid(zw)
    asig = jax.nn.sigmoid(a0_ref[...] + _dot(jnp.dot(mix(4), a1_ref[...],
                                                     preferred_element_type=jnp.float32),
                                             a2_ref[...]))
    gate_ref[0] = _mx(_dot(jax.nn.sigmoid(jnp.dot(mix(5), g1_ref[...],
                                                  preferred_element_type=jnp.float32)),
                           g2_ref[...]))
    xv = mix(2)
    v = jnp.dot(xv, wv_ref[...], preferred_element_type=jnp.float32)
    r = jnp.dot(mix(0), wr_ref[...], preferred_element_type=jnp.float32)

    kk = k * kk_ref[...]
    kk = kk * lax.rsqrt(jnp.maximum(_head_sums(kk * kk, bd_ref[...]), 1e-24))
    kp = k * (1.0 + (asig - 1.0) * ka_ref[...])
    if value_residual:
        mixv = jax.nn.sigmoid(v0_ref[...] + _dot(jnp.dot(xv, v1_ref[...],
                                                         preferred_element_type=jnp.float32),
                                                 v2_ref[...]))
        v = v + (vf_ref[0] - v) * mixv
    else:
        vf_ref[0] = v
    v_ref[0] = _mx(v)

    tri = tri_ref[...]
    hi, mid, lo = _split3(lw)
    c = (jnp.dot(tri, hi, preferred_element_type=jnp.float32)
         + jnp.dot(tri, mid, preferred_element_type=jnp.float32)
         + jnp.dot(tri, lo, preferred_element_type=jnp.float32))
    pinc = jnp.exp(c)
    pinv = jnp.exp(-c)
    rt_ref[0] = _mx(r * pinc)
    at_ref[0] = _mx(-kk * jnp.exp(c - lw))
    bt_ref[0] = _mx(kk * asig * pinv)
    kt_ref[0] = _mx(kp * pinv)
    for ci in range(tm // WKV_CHUNK):
        row = ci * WKV_CHUNK + WKV_CHUNK - 1
        pl_ref[0, ci] = pinc[row:row + 1, :]


def _pad_cols(w, n):
    return jnp.pad(w, ((0, 0), (0, n - w.shape[1])))


def _pad_rows(w, n):
    return jnp.pad(w, ((0, n - w.shape[0]), (0, 0)))


def _lora_width(n):
    return -(-n // LANES) * LANES


def _rwkv_proj(x, g, mu, w_rkv, w0, w1, w2, a0, a1, a2, g1, g2, k_k, k_a, v_res, v_first, *, tm):
    b, s, d = x.shape
    assert s % tm == 0 and tm % WKV_CHUNK == 0
    row = lambda a: a.reshape(1, d)
    cst = lambda shape: pl.BlockSpec(shape, lambda i, j: (0,) * len(shape))
    tok = pl.BlockSpec((1, tm, d), lambda i, j: (i, j, 0))

    def lora(wa, wb):
        n = _lora_width(wa.shape[1])
        return _mx(_pad_cols(wa, n)), _mx(_pad_rows(wb, n))

    w1p, w2p = lora(w1, w2)
    a1p, a2p = lora(a1, a2)
    g1p, g2p = lora(g1, g2)
    idx = jnp.arange(tm)
    tri = ((idx[:, None] >= idx[None, :])
           & (idx[:, None] // WKV_CHUNK == idx[None, :] // WKV_CHUNK)).astype(_MXU_DTYPE)
    bd = _head_block_diag(WKV_GROUP)

    args = [x, row(g), mu, _mx(w_rkv[0]), _mx(w_rkv[1]), _mx(w_rkv[2]), row(w0), w1p, w2p,
            row(a0), a1p, a2p, g1p, g2p, row(k_k), row(k_a), tri, bd]
    in_specs = [tok] + [cst(a.shape) for a in args[1:]]
    if v_res is not None:
        v0, v1, v2 = v_res
        v1p, v2p = lora(v1, v2)
        extra = [row(v0), v1p, v2p]
        args += extra + [v_first]
        in_specs += [cst(a.shape) for a in extra] + [tok]

    nchunk = tm // WKV_CHUNK
    out_specs = [tok] * 6 + [pl.BlockSpec((1, nchunk, 1, d), lambda i, j: (i, j, 0, 0))]
    out_shape = ([jax.ShapeDtypeStruct((b, s, d), _MXU_DTYPE)] * 6
                 + [jax.ShapeDtypeStruct((b, s // WKV_CHUNK, 1, d), jnp.float32)])
    if v_res is None:
        out_specs.append(tok)
        out_shape.append(jax.ShapeDtypeStruct((b, s, d), jnp.float32))
    outs = pl.pallas_call(
        functools.partial(_proj_kernel, value_residual=v_res is not None),
        grid=(b, s // tm),
        in_specs=in_specs,
        out_specs=out_specs,
        out_shape=out_shape,
        scratch_shapes=[pltpu.VMEM((tm + SUBLANES, d), jnp.float32)],
        compiler_params=_params("arbitrary", "arbitrary"),
        name="rwkv_proj",
    )(*args)
    return outs


def _stack(x, bd):
    xb = _mx(x)
    return jnp.concatenate([xb] * (bd.shape[0] // xb.shape[0]), axis=0) * bd


def _wkv_prep(rt_ref, at_ref, bt_ref, kt_ref, v_ref, bd, msk_ref, results):
    L, W = WKV_CHUNK, WKV_PREP_GROUP
    m_strict = msk_ref[0]
    m_incl = msk_ref[1]
    eye = msk_ref[2]
    nlev = msk_ref.shape[0] - 4
    nb, rows, lanes = rt_ref.shape
    st = lambda x: _stack(x, bd)
    for b in range(nb):
        where = [(b, slice(c * L, (c + 1) * L), slice(g * W, (g + 1) * W))
                 for c in range(rows // L) for g in range(lanes // W)]
        n = range(len(where))
        rt = [rt_ref[w] for w in where]
        at = [at_ref[w] for w in where]
        vs = [st(v_ref[w]) for w in where]
        bk = [jnp.concatenate([st(bt_ref[w]), st(kt_ref[w])], axis=0) for w in where]
        yield
        aa = [_dot_nt(jnp.concatenate([at[i], rt[i]], axis=0), bk[i]) for i in n]
        a_ab = [aa[i][:L, :W] for i in n]
        a_ak = [aa[i][:L, W:] * m_strict for i in n]
        a_rk = [aa[i][L:, W:] * m_incl for i in n]
        a_rb = [_mx(aa[i][L:, :W] * m_incl) for i in n]
        yield
        T = [eye + a_ab[i] * msk_ref[3] for i in n]
        for lev in range(nlev):
            w1 = [_dot(a_ab[i] * msk_ref[4 + lev], st(T[i])) for i in n]
            yield
            T = [T[i] + _dot(T[i], st(w1[i])) for i in n]
            yield
        avy = [_dot(jnp.concatenate([a_ak[i], a_rk[i]], axis=0), vs[i]) for i in n]
        yield
        wu = [_dot(T[i], jnp.concatenate([st(at[i]), st(avy[i][:L])], axis=1)) for i in n]
        yield
        for i, w in enumerate(where):
            results.append((w, _mx(wu[i][:, :W]), _mx(wu[i][:, W:]), _mx(avy[i][L:]), a_rb[i]))
        yield


def _wkv_scan(wm_ref, uv_ref, y0_ref, arb_ref, rt_ref, bt_ref, kt_ref, v_ref, gate_ref,
              pl_ref, rk_ref, lnw_ref, lnb_ref, bd, o_ref, g_ref):
    L, W = WKV_CHUNK, WKV_GROUP
    nchunk = rt_ref.shape[1] // L
    chains = [(i, slice(g * W, (g + 1) * W))
              for i in range(rt_ref.shape[0]) for g in range(rt_ref.shape[2] // W)]
    nc = range(len(chains))
    st = lambda x: _stack(x, bd)
    G = [g_ref[n] for n in nc]
    same_head = bd.astype(jnp.float32)
    ys = [[] for _ in chains]
    for c in range(nchunk):
        at_ = [(i, slice(c * L, (c + 1) * L), lanes) for i, lanes in chains]
        ur = [_dot_nt(jnp.concatenate([wm_ref[at_[n]], rt_ref[at_[n]]], axis=0), G[n]) for n in nc]
        bk = [jnp.concatenate([bt_ref[at_[n]], kt_ref[at_[n]]], axis=0) for n in nc]
        yield
        u = [_mx(ur[n][:L] + uv_ref[at_[n]]) for n in nc]
        upd = [_dot_tn(jnp.concatenate([u[n], v_ref[at_[n]]], axis=0), bk[n]) for n in nc]
        for n in nc:
            ys[n].append(ur[n][L:] + _dot(arb_ref[at_[n]], st(u[n])) + y0_ref[at_[n]])
        yield
        G = [(G[n] + upd[n] * same_head) * pl_ref[chains[n][0], c][:, chains[n][1]] for n in nc]
    for n in nc:
        g_ref[n] = G[n]

    for n, (i, lanes) in enumerate(chains):
        y = jnp.concatenate(ys[n], axis=0)
        mean = _dot(y, bd) * (1.0 / HEAD_SIZE)
        yc = y - mean
        var = _dot(yc * yc, bd) * (1.0 / HEAD_SIZE)
        yn = yc * lax.rsqrt(var + GN_EPS) * lnw_ref[:, lanes] + lnb_ref[:, lanes]
        rkk = rt_ref[i, :, lanes].astype(jnp.float32) * kt_ref[i, :, lanes] * rk_ref[:, lanes]
        bonus = _dot(rkk, bd) * v_ref[i, :, lanes]
        o_ref[i, :, lanes] = ((yn + bonus) * gate_ref[i, :, lanes]).astype(o_ref.dtype)
        yield


def _wkv_kernel(rtp_ref, atp_ref, btp_ref, ktp_ref, vp_ref,
                rt_ref, bt_ref, kt_ref, v_ref, gate_ref, pl_ref,
                rk_ref, lnw_ref, lnb_ref, bdp_ref, msk_ref, bd_ref,
                o_ref, wm_ref, uv_ref, y0_ref, arb_ref, g_ref):
    k = pl.program_id(1)

    @pl.when(k == 0)
    def _():
        for ref in (wm_ref, uv_ref, y0_ref, arb_ref):
            ref[...] = jnp.zeros_like(ref)

    @pl.when(k <= 1)
    def _():
        g_ref[...] = jnp.zeros_like(g_ref)

    results = []
    scan = _wkv_scan(wm_ref, uv_ref, y0_ref, arb_ref, rt_ref, bt_ref, kt_ref, v_ref, gate_ref,
                     pl_ref, rk_ref, lnw_ref, lnb_ref, bd_ref[...], o_ref, g_ref)
    prep = _wkv_prep(rtp_ref, atp_ref, btp_ref, ktp_ref, vp_ref, bdp_ref[...], msk_ref, results)
    scan_live = prep_live = True
    while scan_live or prep_live:
        if scan_live:
            scan_live = next(scan, _DONE) is not _DONE
        for _ in range(WKV_PREP_STAGES_PER_SCAN_STAGE):
            if prep_live:
                prep_live = next(prep, _DONE) is not _DONE
    for w, wm, uv, y0, arb in results:
        wm_ref[w], uv_ref[w], y0_ref[w], arb_ref[w] = wm, uv, y0, arb


def _wkv_masks(lanes):
    L = WKV_CHUNK
    t = jnp.arange(L)[:, None]
    s = (jnp.arange(lanes) % L)[None, :]
    masks = [s < t, s <= t, s == t]
    n = 1
    while n < L:
        masks.append((s < t) & (t // (2 * n) == s // (2 * n)) & (t // n != s // n))
        n *= 2
    return jnp.stack(masks).astype(jnp.float32)


def _head_block_diag(lanes):
    head = jnp.arange(lanes) // HEAD_SIZE
    return (head[:, None] == head[None, :]).astype(_MXU_DTYPE)


def _wkv(rt, at, bt, kt, v, gate, pl_arr, r_k, ln_w, ln_b):
    b, s, d = rt.shape
    assert WKV_CHUNK == HEAD_SIZE
    bd_prep = _head_block_diag(WKV_PREP_GROUP)
    masks = _wkv_masks(WKV_PREP_GROUP)
    bd = _head_block_diag(WKV_GROUP)
    blk = min(WKV_BLOCK, s)
    assert s % blk == 0 and blk % WKV_CHUNK == 0
    nblk = s // blk
    nb = WKV_BATCH if b % WKV_BATCH == 0 else 1
    cur = pl.BlockSpec((nb, blk, d), lambda i, k: (i, jnp.minimum(k, nblk - 1), 0))
    prev = pl.BlockSpec((nb, blk, d), lambda i, k: (i, jnp.maximum(k - 1, 0), 0))
    cst = lambda a: pl.BlockSpec(a.shape, lambda i, k: (0,) * a.ndim)
    rows = [a.reshape(1, d) for a in (r_k, ln_w, ln_b)]
    consts = rows + [bd_prep, masks, bd]
    return pl.pallas_call(
        _wkv_kernel,
        grid=(b // nb, nblk + 1),
        in_specs=[cur] * 5 + [prev] * 5 + [
            pl.BlockSpec((nb, blk // WKV_CHUNK, 1, d), lambda i, k: (i, jnp.maximum(k - 1, 0), 0, 0)),
        ] + [cst(a) for a in consts],
        out_specs=prev,
        out_shape=jax.ShapeDtypeStruct((b, s, d), _MXU_DTYPE),
        scratch_shapes=[pltpu.VMEM((nb, blk, d), _MXU_DTYPE)] * 4
        + [pltpu.VMEM((nb * (d // WKV_GROUP), WKV_GROUP, WKV_GROUP), jnp.float32)],
        compiler_params=_params("arbitrary", "arbitrary"),
        name="wkv7",
    )(rt, at, bt, kt, v, rt, bt, kt, v, gate, pl_arr, *consts)


def _tile(n, want):
    return min(n, want)


def kernel(x, norm_g, final_g, ffn_w_gu, ffn_w_d, conv_w_in, conv_w, conv_w_out, rwkv_mu, rwkv_w_rkv, rwkv_w0, rwkv_w1, rwkv_w2, rwkv_a0, rwkv_a1, rwkv_a2, rwkv_g1, rwkv_g2, rwkv_k_k, rwkv_k_a, rwkv_r_k, rwkv_ln_w, rwkv_ln_b, rwkv_w_o, rwkv_v0, rwkv_v1, rwkv_v2):
    b, s, d = x.shape
    depth = norm_g.shape[0]
    f = ffn_w_d.shape[2]
    assert d % WKV_GROUP == 0 and WKV_GROUP % WKV_PREP_GROUP == 0
    assert s % WKV_CHUNK == 0
    t = b * s
    ffn_tm = _tile(t, 1024)
    w_gu = _mx(ffn_w_gu)
    w_d = _mx(ffn_w_d)

    def ffn(x3, i, half, mix=None, final_norm=False):
        y2, w_o = (None, None) if mix is None else (mix[0].reshape(t, d), mix[1])
        y = _ffn(x3.reshape(t, d), norm_g[i, 2 * half], w_gu, w_d, (i, half), final_g,
                 y2, w_o, final_norm=final_norm, tm=ffn_tm)
        return y.reshape(b, s, d)

    v_first = None
    for i in range(depth):
        x = ffn(x, i, 0)
        if i % 2 == 0:
            c = i // 2
            y = _conv_mixer(x, norm_g[i, 1], _mx(conv_w_in[c]), conv_w[c], tm=_tile(s, 1024))
            w_o = _mx(conv_w_out[c])
        else:
            j = i // 2
            v_res = None if j == 0 else (rwkv_v0[j - 1], rwkv_v1[j - 1], rwkv_v2[j - 1])
            outs = _rwkv_proj(
                x, norm_g[i, 1], rwkv_mu[j], rwkv_w_rkv[j], rwkv_w0[j], rwkv_w1[j], rwkv_w2[j],
                rwkv_a0[j], rwkv_a1[j], rwkv_a2[j], rwkv_g1[j], rwkv_g2[j],
                rwkv_k_k[j], rwkv_k_a[j], v_res, v_first, tm=_tile(s, 512))
            rt, at, bt, kt, v, gate, pl_arr = outs[:7]
            if j == 0:
                v_first = outs[7]
            y = _wkv(rt, at, bt, kt, v, gate, pl_arr, rwkv_r_k[j].reshape(-1),
                     rwkv_ln_w[j], rwkv_ln_b[j])
            w_o = _mx(rwkv_w_o[j])
        x = ffn(x, i, 1, mix=(y, w_o), final_norm=(i == depth - 1))
    return x
```

```python
import functools
import math

import jax
import jax.numpy as jnp
from jax import lax
from jax.experimental import pallas as pl
from jax.experimental.pallas import tpu as pltpu

NORM_EPS = 1e-5
GN_EPS = 64e-5
HEAD_SIZE = 64
WKV_CHUNK = 64
WKV_GROUP = 256
WKV_PREP_GROUP = 128
WKV_SUB = 16
WKV_BLOCK = 256
WKV_BATCH = 2
WKV_PREP_STAGES_PER_SCAN_STAGE = 2
FFN_CHUNK = 512
FFN_PROLOGUE_BLOCKS = 4
SUBLANES = 8
LANES = 128
DECAY_SCALE = math.exp(-0.5)
VMEM_LIMIT = 56 * 1024 * 1024

_MXU_DTYPE = jnp.bfloat16
_DONE = object()


def _mx(x):
    return x.astype(_MXU_DTYPE)


def _dot(a, b):
    return jnp.dot(_mx(a), _mx(b), preferred_element_type=jnp.float32)


def _dot_nt(a, b):
    return lax.dot_general(_mx(a), _mx(b), (((1,), (1,)), ((), ())),
                           preferred_element_type=jnp.float32)


def _dot_tn(a, b):
    return lax.dot_general(_mx(a), _mx(b), (((0,), (0,)), ((), ())),
                           preferred_element_type=jnp.float32)


def _rms(x, g):
    ms = jnp.mean(x * x, axis=-1, keepdims=True)
    return x * lax.rsqrt(ms + NORM_EPS) * g


def _params(*sem):
    return pltpu.CompilerParams(dimension_semantics=sem, vmem_limit_bytes=VMEM_LIMIT)


def _ffn_kernel(*refs, mixer_out, final_norm, chunks):
    if mixer_out:
        x_ref, y_ref, wo_ref, g_ref, wgu_ref, wd_ref, fg_ref, o_ref, act_ref = refs
        tm = x_ref.shape[0]
        step = tm // FFN_PROLOGUE_BLOCKS if tm % (FFN_PROLOGUE_BLOCKS * 2 * SUBLANES) == 0 else tm
        xs, hs = [], []
        for lo in range(0, tm, step):
            xs.append(x_ref[lo:lo + step, :] + jnp.dot(y_ref[lo:lo + step, :], wo_ref[...],
                                                       preferred_element_type=jnp.float32))
            hs.append(_mx(_rms(xs[-1], g_ref[...])))
        x = jnp.concatenate(xs, axis=0)
        h = jnp.concatenate(hs, axis=0)
    else:
        x_ref, g_ref, wgu_ref, wd_ref, fg_ref, o_ref, act_ref = refs
        x = x_ref[...]
        h = _mx(_rms(x, g_ref[...]))
    f = wd_ref.shape[0]
    for lo, hi in chunks:
        gate = jnp.dot(h, wgu_ref[:, lo:hi], preferred_element_type=jnp.float32)
        up = jnp.dot(h, wgu_ref[:, f + lo:f + hi], preferred_element_type=jnp.float32)
        act_ref[:, lo:hi] = _mx(gate * jax.nn.sigmoid(gate) * up)
    y = x + 0.5 * jnp.dot(act_ref[...], wd_ref[...], preferred_element_type=jnp.float32)
    if final_norm:
        y = _rms(y, fg_ref[...])
    o_ref[...] = y


def _resident(shape):
    return pl.BlockSpec(shape, lambda i: (0,) * len(shape), pipeline_mode=pl.Buffered(1))


def _ffn(x2, g, w_gu, w_d, which, final_g, y2=None, w_o=None, *, final_norm, tm):
    t, d = x2.shape
    assert t % tm == 0
    f = w_d.shape[2]
    chunks = tuple((lo, min(lo + FFN_CHUNK, f)) for lo in range(0, f, FFN_CHUNK))
    tok = pl.BlockSpec((tm, d), lambda i: (i, 0))
    layer_weights = lambda rows, cols: pl.BlockSpec((None, None, rows, cols),
                                                    lambda i: (*which, 0, 0),
                                                    pipeline_mode=pl.Buffered(1))
    mixer_out = y2 is not None
    args, in_specs = [x2], [tok]
    if mixer_out:
        args += [y2, w_o]
        in_specs += [tok, _resident((d, d))]
    args += [g.reshape(1, d), w_gu, w_d, final_g.reshape(1, d)]
    in_specs += [_resident((1, d)), layer_weights(d, 2 * f), layer_weights(f, d), _resident((1, d))]
    return pl.pallas_call(
        functools.partial(_ffn_kernel, mixer_out=mixer_out, final_norm=final_norm, chunks=chunks),
        grid=(t // tm,),
        in_specs=in_specs,
        out_specs=tok,
        out_shape=jax.ShapeDtypeStruct((t, d), jnp.float32),
        scratch_shapes=[pltpu.VMEM((tm, f), _MXU_DTYPE)],
        compiler_params=_params("parallel"),
        name="ffn",
    )(*args)


def _conv_kernel(x_ref, g_ref, win_ref, cw_ref, o_ref, zbuf_ref):
    tm, d = x_ref.shape[1], x_ref.shape[2]

    @pl.when(pl.program_id(1) == 0)
    def _():
        zbuf_ref[0:SUBLANES, :] = jnp.zeros((SUBLANES, d), jnp.float32)

    h = _mx(_rms(x_ref[0], g_ref[...]))
    gc = jnp.dot(h, win_ref[:, d:2 * d], preferred_element_type=jnp.float32)
    hh = jnp.dot(h, win_ref[:, 2 * d:3 * d], preferred_element_type=jnp.float32)
    z = gc * hh
    zbuf_ref[SUBLANES:SUBLANES + tm, :] = z
    z1 = zbuf_ref[SUBLANES - 1:SUBLANES - 1 + tm, :]
    z2 = zbuf_ref[SUBLANES - 2:SUBLANES - 2 + tm, :]
    conv = cw_ref[2:3, :] * z + cw_ref[1:2, :] * z1 + cw_ref[0:1, :] * z2
    zbuf_ref[0:SUBLANES, :] = z[tm - SUBLANES:tm, :]
    gb = jnp.dot(h, win_ref[:, 0:d], preferred_element_type=jnp.float32)
    o_ref[0] = _mx(gb * conv)


def _conv_mixer(x, g, w_in, conv_w, *, tm):
    b, s, d = x.shape
    assert s % tm == 0
    return pl.pallas_call(
        _conv_kernel,
        grid=(b, s // tm),
        in_specs=[
            pl.BlockSpec((1, tm, d), lambda i, j: (i, j, 0)),
            pl.BlockSpec((1, d), lambda i, j: (0, 0)),
            pl.BlockSpec((d, 3 * d), lambda i, j: (0, 0)),
            pl.BlockSpec(conv_w.shape, lambda i, j: (0, 0)),
        ],
        out_specs=pl.BlockSpec((1, tm, d), lambda i, j: (i, j, 0)),
        out_shape=jax.ShapeDtypeStruct((b, s, d), _MXU_DTYPE),
        scratch_shapes=[pltpu.VMEM((tm + SUBLANES, d), jnp.float32)],
        compiler_params=_params("arbitrary", "arbitrary"),
        name="conv_mixer",
    )(x, g.reshape(1, d), w_in, conv_w)


def _head_sums(x, bd):
    d = x.shape[1]
    parts = [_dot(x[:, c:c + WKV_GROUP], bd) for c in range(0, d, WKV_GROUP)]
    return jnp.concatenate(parts, axis=1)


def _split3(x):
    hi = _mx(x)
    r1 = x - hi.astype(jnp.float32)
    mid = _mx(r1)
    lo = _mx(r1 - mid.astype(jnp.float32))
    return hi, mid, lo


def _proj_kernel(*refs, value_residual):
    if value_residual:
        (x_ref, g_ref, mu_ref, wr_ref, wk_ref, wv_ref, w0_ref, w1_ref, w2_ref,
         a0_ref, a1_ref, a2_ref, g1_ref, g2_ref, kk_ref, ka_ref, tri_ref, bd_ref,
         v0_ref, v1_ref, v2_ref, vf_ref,
         rt_ref, at_ref, bt_ref, kt_ref, v_ref, gate_ref, pl_ref, hbuf_ref) = refs
    else:
        (x_ref, g_ref, mu_ref, wr_ref, wk_ref, wv_ref, w0_ref, w1_ref, w2_ref,
         a0_ref, a1_ref, a2_ref, g1_ref, g2_ref, kk_ref, ka_ref, tri_ref, bd_ref,
         rt_ref, at_ref, bt_ref, kt_ref, v_ref, gate_ref, pl_ref, vf_ref, hbuf_ref) = refs
    tm, d = x_ref.shape[1], x_ref.shape[2]

    @pl.when(pl.program_id(1) == 0)
    def _():
        hbuf_ref[0:SUBLANES, :] = jnp.zeros((SUBLANES, d), jnp.float32)

    h = _rms(x_ref[0], g_ref[...])
    hbuf_ref[SUBLANES:SUBLANES + tm, :] = h
    xx = hbuf_ref[SUBLANES - 1:SUBLANES - 1 + tm, :] - h
    hbuf_ref[0:SUBLANES, :] = h[tm - SUBLANES:tm, :]

    def mix(p):
        return _mx(h + xx * mu_ref[p:p + 1, :])

    k = jnp.dot(mix(1), wk_ref[...], preferred_element_type=jnp.float32)
    zw = w0_ref[...] + _dot(jnp.tanh(jnp.dot(mix(3), w1_ref[...],
                                             preferred_element_type=jnp.float32)), w2_ref[...])
    lw = -DECAY_SCALE * jax.nn.sigmoid(zw)
    asig = jax.nn.sigmoid(a0_ref[...] + _dot(jnp.dot(mix(4), a1_ref[...],
                                                     preferred_element_type=jnp.float32),
                                             a2_ref[...]))
    gate_ref[0] = _mx(_dot(jax.nn.sigmoid(jnp.dot(mix(5), g1_ref[...],
                                                  preferred_element_type=jnp.float32)),
                           g2_ref[...]))
    xv = mix(2)
    v = jnp.dot(xv, wv_ref[...], preferred_element_type=jnp.float32)
    r = jnp.dot(mix(0), wr_ref[...], preferred_element_type=jnp.float32)

    kk = k * kk_ref[...]
    kk = kk * lax.rsqrt(jnp.maximum(_head_sums(kk * kk, bd_ref[...]), 1e-24))
    kp = k * (1.0 + (asig - 1.0) * ka_ref[...])
    if value_residual:
        mixv = jax.nn.sigmoid(v0_ref[...] + _dot(jnp.dot(xv, v1_ref[...],
                                                         preferred_element_type=jnp.float32),
                                                 v2_ref[...]))
        v = v + (vf_ref[0] - v) * mixv
    else:
        vf_ref[0] = v
    v_ref[0] = _mx(v)

    tri = tri_ref[...]
    hi, mid, lo = _split3(lw)
    c = (jnp.dot(tri, hi, preferred_element_type=jnp.float32)
         + jnp.dot(tri, mid, preferred_element_type=jnp.float32)
         + jnp.dot(tri, lo, preferred_element_type=jnp.float32))
    pinc = jnp.exp(c)
    pinv = jnp.exp(-c)
    rt_ref[0] = _mx(r * pinc)
    at_ref[0] = _mx(-kk * jnp.exp(c - lw))
    bt_ref[0] = _mx(kk * asig * pinv)
    kt_ref[0] = _mx(kp * pinv)
    for ci in range(tm // WKV_CHUNK):
        row = ci * WKV_CHUNK + WKV_CHUNK - 1
        pl_ref[0, ci] = pinc[row:row + 1, :]


def _pad_cols(w, n):
    return jnp.pad(w, ((0, 0), (0, n - w.shape[1])))


def _pad_rows(w, n):
    return jnp.pad(w, ((0, n - w.shape[0]), (0, 0)))


def _lora_width(n):
    return -(-n // LANES) * LANES


def _rwkv_proj(x, g, mu, w_rkv, w0, w1, w2, a0, a1, a2, g1, g2, k_k, k_a, v_res, v_first, *, tm):
    b, s, d = x.shape
    assert s % tm == 0 and tm % WKV_CHUNK == 0
    row = lambda a: a.reshape(1, d)
    cst = lambda shape: pl.BlockSpec(shape, lambda i, j: (0,) * len(shape))
    tok = pl.BlockSpec((1, tm, d), lambda i, j: (i, j, 0))

    def lora(wa, wb):
        n = _lora_width(wa.shape[1])
        return _mx(_pad_cols(wa, n)), _mx(_pad_rows(wb, n))

    w1p, w2p = lora(w1, w2)
    a1p, a2p = lora(a1, a2)
    g1p, g2p = lora(g1, g2)
    idx = jnp.arange(tm)
    tri = ((idx[:, None] >= idx[None, :])
           & (idx[:, None] // WKV_CHUNK == idx[None, :] // WKV_CHUNK)).astype(_MXU_DTYPE)
    bd = _head_block_diag(WKV_GROUP)

    args = [x, row(g), mu, _mx(w_rkv[0]), _mx(w_rkv[1]), _mx(w_rkv[2]), row(w0), w1p, w2p,
            row(a0), a1p, a2p, g1p, g2p, row(k_k), row(k_a), tri, bd]
    in_specs = [tok] + [cst(a.shape) for a in args[1:]]
    if v_res is not None:
        v0, v1, v2 = v_res
        v1p, v2p = lora(v1, v2)
        extra = [row(v0), v1p, v2p]
        args += extra + [v_first]
        in_specs += [cst(a.shape) for a in extra] + [tok]

    nchunk = tm // WKV_CHUNK
    out_specs = [tok] * 6 + [pl.BlockSpec((1, nchunk, 1, d), lambda i, j: (i, j, 0, 0))]
    out_shape = ([jax.ShapeDtypeStruct((b, s, d), _MXU_DTYPE)] * 6
                 + [jax.ShapeDtypeStruct((b, s // WKV_CHUNK, 1, d), jnp.float32)])
    if v_res is None:
        out_specs.append(tok)
        out_shape.append(jax.ShapeDtypeStruct((b, s, d), jnp.float32))
    outs = pl.pallas_call(
        functools.partial(_proj_kernel, value_residual=v_res is not None),
        grid=(b, s // tm),
        in_specs=in_specs,
        out_specs=out_specs,
        out_shape=out_shape,
        scratch_shapes=[pltpu.VMEM((tm + SUBLANES, d), jnp.float32)],
        compiler_params=_params("arbitrary", "arbitrary"),
        name="rwkv_proj",
    )(*args)
    return outs


def _stack(x, bd):
    xb = _mx(x)
    return jnp.concatenate([xb] * (bd.shape[0] // xb.shape[0]), axis=0) * bd


def _wkv_prep(rt_ref, at_ref, bt_ref, kt_ref, v_ref, bd, bds, msk_ref, sub_ref, results):
    L, W = WKV_CHUNK, WKV_PREP_GROUP
    m_strict = msk_ref[0]
    m_incl = msk_ref[1]
    eye = msk_ref[2]
    nlev = msk_ref.shape[0] - 4
    nsub = sub_ref.shape[0] - (L // WKV_SUB) - 2
    nb, rows, lanes = rt_ref.shape
    st = lambda x: _stack(x, bd)
    for b in range(nb):
        where = [(b, slice(c * L, (c + 1) * L), slice(g * W, (g + 1) * W))
                 for c in range(rows // L) for g in range(lanes // W)]
        n = range(len(where))
        rt = [rt_ref[w] for w in where]
        at = [at_ref[w] for w in where]
        vs = [st(v_ref[w]) for w in where]
        bk = [jnp.concatenate([st(bt_ref[w]), st(kt_ref[w])], axis=0) for w in where]
        yield
        aa = [_dot_nt(jnp.concatenate([at[i], rt[i]], axis=0), bk[i]) for i in n]
        a_ab = [aa[i][:L, :W] for i in n]
        a_ak = [aa[i][:L, W:] * m_strict for i in n]
        a_rk = [aa[i][L:, W:] * m_incl for i in n]
        a_rb = [_mx(aa[i][L:, :W] * m_incl) for i in n]
        yield
        S = WKV_SUB
        slabs = range(L // S)
        cab = [sum(a_ab[i][j * S:(j + 1) * S] * sub_ref[j] for j in slabs) for i in n]
        ct = [sub_ref[len(slabs)] + cab[i] * sub_ref[len(slabs) + 1] for i in n]
        for lev in range(nsub):
            w1 = [_dot(cab[i] * sub_ref[len(slabs) + 2 + lev], _stack(ct[i], bds)) for i in n]
            yield
            ct = [ct[i] + _dot(ct[i], _stack(w1[i], bds)) for i in n]
            yield
        T = [jnp.concatenate([ct[i] * sub_ref[j] for j in slabs], axis=0) for i in n]
        for lev in range(nsub, nlev):
            w1 = [_dot(a_ab[i] * msk_ref[4 + lev], st(T[i])) for i in n]
            yield
            T = [T[i] + _dot(T[i], st(w1[i])) for i in n]
            yield
        avy = [_dot(jnp.concatenate([a_ak[i], a_rk[i]], axis=0), vs[i]) for i in n]
        yield
        wu = [_dot(T[i], jnp.concatenate([st(at[i]), st(avy[i][:L])], axis=1)) for i in n]
        yield
        for i, w in enumerate(where):
            results.append((w, _mx(wu[i][:, :W]), _mx(wu[i][:, W:]), _mx(avy[i][L:]), a_rb[i]))
        yield


def _wkv_scan(wm_ref, uv_ref, y0_ref, arb_ref, rt_ref, bt_ref, kt_ref, v_ref, gate_ref,
              pl_ref, rk_ref, lnw_ref, lnb_ref, bd, o_ref, g_ref):
    L, W = WKV_CHUNK, WKV_GROUP
    nchunk = rt_ref.shape[1] // L
    chains = [(i, slice(g * W, (g + 1) * W))
              for i in range(rt_ref.shape[0]) for g in range(rt_ref.shape[2] // W)]
    nc = range(len(chains))
    st = lambda x: _stack(x, bd)
    G = [g_ref[n] for n in nc]
    same_head = bd.astype(jnp.float32)
    ys = [[] for _ in chains]
    for c in range(nchunk):
        at_ = [(i, slice(c * L, (c + 1) * L), lanes) for i, lanes in chains]
        ur = [_dot_nt(jnp.concatenate([wm_ref[at_[n]], rt_ref[at_[n]]], axis=0), G[n]) for n in nc]
        bk = [jnp.concatenate([bt_ref[at_[n]], kt_ref[at_[n]]], axis=0) for n in nc]
        yield
        u = [_mx(ur[n][:L] + uv_ref[at_[n]]) for n in nc]
        upd = [_dot_tn(jnp.concatenate([u[n], v_ref[at_[n]]], axis=0), bk[n]) for n in nc]
        for n in nc:
            ys[n].append(ur[n][L:] + _dot(arb_ref[at_[n]], st(u[n])) + y0_ref[at_[n]])
        yield
        G = [(G[n] + upd[n] * same_head) * pl_ref[chains[n][0], c][:, chains[n][1]] for n in nc]
    for n in nc:
        g_ref[n] = G[n]

    for n, (i, lanes) in enumerate(chains):
        y = jnp.concatenate(ys[n], axis=0)
        mean = _dot(y, bd) * (1.0 / HEAD_SIZE)
        yc = y - mean
        var = _dot(yc * yc, bd) * (1.0 / HEAD_SIZE)
        yn = yc * lax.rsqrt(var + GN_EPS) * lnw_ref[:, lanes] + lnb_ref[:, lanes]
        rkk = rt_ref[i, :, lanes].astype(jnp.float32) * kt_ref[i, :, lanes] * rk_ref[:, lanes]
        bonus = _dot(rkk, bd) * v_ref[i, :, lanes]
        o_ref[i, :, lanes] = ((yn + bonus) * gate_ref[i, :, lanes]).astype(o_ref.dtype)
        yield


def _wkv_kernel(rtp_ref, atp_ref, btp_ref, ktp_ref, vp_ref,
                rt_ref, bt_ref, kt_ref, v_ref, gate_ref, pl_ref,
                rk_ref, lnw_ref, lnb_ref, bdp_ref, bds_ref, msk_ref, sub_ref, bd_ref,
                o_ref, wm_ref, uv_ref, y0_ref, arb_ref, g_ref):
    k = pl.program_id(1)

    @pl.when(k == 0)
    def _():
        for ref in (wm_ref, uv_ref, y0_ref, arb_ref):
            ref[...] = jnp.zeros_like(ref)

    @pl.when(k <= 1)
    def _():
        g_ref[...] = jnp.zeros_like(g_ref)

    results = []
    scan = _wkv_scan(wm_ref, uv_ref, y0_ref, arb_ref, rt_ref, bt_ref, kt_ref, v_ref, gate_ref,
                     pl_ref, rk_ref, lnw_ref, lnb_ref, bd_ref[...], o_ref, g_ref)
    prep = _wkv_prep(rtp_ref, atp_ref, btp_ref, ktp_ref, vp_ref, bdp_ref[...], bds_ref[...],
                     msk_ref, sub_ref, results)
    scan_live = prep_live = True
    while scan_live or prep_live:
        if scan_live:
            scan_live = next(scan, _DONE) is not _DONE
        for _ in range(WKV_PREP_STAGES_PER_SCAN_STAGE):
            if prep_live:
                prep_live = next(prep, _DONE) is not _DONE
    for w, wm, uv, y0, arb in results:
        wm_ref[w], uv_ref[w], y0_ref[w], arb_ref[w] = wm, uv, y0, arb


def _wkv_masks(lanes):
    L = WKV_CHUNK
    t = jnp.arange(L)[:, None]
    s = (jnp.arange(lanes) % L)[None, :]
    masks = [s < t, s <= t, s == t]
    n = 1
    while n < L:
        masks.append((s < t) & (t // (2 * n) == s // (2 * n)) & (t // n != s // n))
        n *= 2
    return jnp.stack(masks).astype(jnp.float32)


def _wkv_sub_masks(lanes):
    S = WKV_SUB
    t = jnp.arange(S)[:, None]
    lane = jnp.arange(lanes)[None, :]
    s = lane % S
    slab = (lane % WKV_CHUNK) // S
    masks = [jnp.broadcast_to(slab == j, (S, lanes)) for j in range(WKV_CHUNK // S)]
    masks.append(s == t)
    n = 1
    while n < S:
        masks.append((s < t) & (t // (2 * n) == s // (2 * n)) & (t // n != s // n))
        n *= 2
    return jnp.stack(masks).astype(jnp.float32)


def _block_diag(lanes, size):
    blk = jnp.arange(lanes) // size
    return (blk[:, None] == blk[None, :]).astype(_MXU_DTYPE)


def _head_block_diag(lanes):
    head = jnp.arange(lanes) // HEAD_SIZE
    return (head[:, None] == head[None, :]).astype(_MXU_DTYPE)


def _wkv(rt, at, bt, kt, v, gate, pl_arr, r_k, ln_w, ln_b):
    b, s, d = rt.shape
    assert WKV_CHUNK == HEAD_SIZE
    bd_prep = _head_block_diag(WKV_PREP_GROUP)
    masks = _wkv_masks(WKV_PREP_GROUP)
    sub_masks = _wkv_sub_masks(WKV_PREP_GROUP)
    bd_sub = _block_diag(WKV_PREP_GROUP, WKV_SUB)
    bd = _head_block_diag(WKV_GROUP)
    blk = min(WKV_BLOCK, s)
    assert s % blk == 0 and blk % WKV_CHUNK == 0
    nblk = s // blk
    nb = WKV_BATCH if b % WKV_BATCH == 0 else 1
    cur = pl.BlockSpec((nb, blk, d), lambda i, k: (i, jnp.minimum(k, nblk - 1), 0))
    prev = pl.BlockSpec((nb, blk, d), lambda i, k: (i, jnp.maximum(k - 1, 0), 0))
    cst = lambda a: pl.BlockSpec(a.shape, lambda i, k: (0,) * a.ndim)
    rows = [a.reshape(1, d) for a in (r_k, ln_w, ln_b)]
    consts = rows + [bd_prep, bd_sub, masks, sub_masks, bd]
    return pl.pallas_call(
        _wkv_kernel,
        grid=(b // nb, nblk + 1),
        in_specs=[cur] * 5 + [prev] * 5 + [
            pl.BlockSpec((nb, blk // WKV_CHUNK, 1, d), lambda i, k: (i, jnp.maximum(k - 1, 0), 0, 0)),
        ] + [cst(a) for a in consts],
        out_specs=prev,
        out_shape=jax.ShapeDtypeStruct((b, s, d), _MXU_DTYPE),
        scratch_shapes=[pltpu.VMEM((nb, blk, d), _MXU_DTYPE)] * 4
        + [pltpu.VMEM((nb * (d // WKV_GROUP), WKV_GROUP, WKV_GROUP), jnp.float32)],
        compiler_params=_params("arbitrary", "arbitrary"),
        name="wkv7",
    )(rt, at, bt, kt, v, rt, bt, kt, v, gate, pl_arr, *consts)


def _tile(n, want):
    return min(n, want)


def kernel(x, norm_g, final_g, ffn_w_gu, ffn_w_d, conv_w_in, conv_w, conv_w_out, rwkv_mu, rwkv_w_rkv, rwkv_w0, rwkv_w1, rwkv_w2, rwkv_a0, rwkv_a1, rwkv_a2, rwkv_g1, rwkv_g2, rwkv_k_k, rwkv_k_a, rwkv_r_k, rwkv_ln_w, rwkv_ln_b, rwkv_w_o, rwkv_v0, rwkv_v1, rwkv_v2):
    b, s, d = x.shape
    depth = norm_g.shape[0]
    f = ffn_w_d.shape[2]
    assert d % WKV_GROUP == 0 and WKV_GROUP % WKV_PREP_GROUP == 0
    assert s % WKV_CHUNK == 0
    t = b * s
    ffn_tm = _tile(t, 1024)
    w_gu = _mx(ffn_w_gu)
    w_d = _mx(ffn_w_d)

    def ffn(x3, i, half, mix=None, final_norm=False):
        y2, w_o = (None, None) if mix is None else (mix[0].reshape(t, d), mix[1])
        y = _ffn(x3.reshape(t, d), norm_g[i, 2 * half], w_gu, w_d, (i, half), final_g,
                 y2, w_o, final_norm=final_norm, tm=ffn_tm)
        return y.reshape(b, s, d)

    v_first = None
    for i in range(depth):
        x = ffn(x, i, 0)
        if i % 2 == 0:
            c = i // 2
            y = _conv_mixer(x, norm_g[i, 1], _mx(conv_w_in[c]), conv_w[c], tm=_tile(s, 1024))
            w_o = _mx(conv_w_out[c])
        else:
            j = i // 2
            v_res = None if j == 0 else (rwkv_v0[j - 1], rwkv_v1[j - 1], rwkv_v2[j - 1])
            outs = _rwkv_proj(
                x, norm_g[i, 1], rwkv_mu[j], rwkv_w_rkv[j], rwkv_w0[j], rwkv_w1[j], rwkv_w2[j],
                rwkv_a0[j], rwkv_a1[j], rwkv_a2[j], rwkv_g1[j], rwkv_g2[j],
                rwkv_k_k[j], rwkv_k_a[j], v_res, v_first, tm=_tile(s, 512))
            rt, at, bt, kt, v, gate, pl_arr = outs[:7]
            if j == 0:
                v_first = outs[7]
            y = _wkv(rt, at, bt, kt, v, gate, pl_arr, rwkv_r_k[j].reshape(-1),
                     rwkv_ln_w[j], rwkv_ln_b[j])
            w_o = _mx(rwkv_w_o[j])
        x = ffn(x, i, 1, mix=(y, w_o), final_norm=(i == depth - 1))
    return x
```
